```python
import math
import jax
import jax.numpy as jnp
from jax import lax
import numpy as np

D_MODEL = 1024
BATCH = 8
SEQ = 4096
DEPTH = 2

NORM_EPS = 1e-6
D_FF = 2816
CONV_CH = 1024
CONV_WIDTH = 31
SSM_D_INNER = 1024
SSM_HEAD_DIM = 64
SSM_HEADS = SSM_D_INNER // SSM_HEAD_DIM
SSM_GROUPS = 2
SSM_STATE = 128
SSM_CONV_WIDTH = 4
SSM_CHUNK = 128
SSM_XBC = SSM_D_INNER + 2 * SSM_GROUPS * SSM_STATE
HYB_IN_COLS = 2 * CONV_CH + SSM_D_INNER + SSM_XBC + SSM_HEADS
HYB_MIX_WIDTH = CONV_CH + SSM_D_INNER
ATTN_HEAD_DIM = 64
ATTN_HEADS = 8
DILATED_PATTERNS = ((128, 1), (512, 4), (2048, 16))
ATTN_GROUPS = len(DILATED_PATTERNS)
ATTN_BLOCK = 128
ATTN_QKV_COLS = 3 * ATTN_GROUPS * ATTN_HEADS * ATTN_HEAD_DIM
ATTN_OUT_WIDTH = ATTN_HEADS * ATTN_HEAD_DIM
N_EVEN = (DEPTH + 1) // 2
N_ODD = DEPTH // 2

kernel_name = 'hybrid_conv_ssd_dilated_attn_macaron'


def rms_norm(x, g):
    xf = x.astype(jnp.float32)
    y = xf * lax.rsqrt(jnp.mean(xf * xf, axis=-1, keepdims=True) + NORM_EPS)
    return (y * g.astype(jnp.float32)).astype(x.dtype)


def layer_norm(x, g, b):
    xf = x.astype(jnp.float32)
    xc = xf - jnp.mean(xf, axis=-1, keepdims=True)
    y = xc * lax.rsqrt(jnp.mean(xc * xc, axis=-1, keepdims=True) + NORM_EPS)
    return (y * g.astype(jnp.float32) + b.astype(jnp.float32)).astype(x.dtype)


def swiglu(u, w1, w2):
    gate, up = jnp.split(u @ w1, 2, axis=-1)
    return (jax.nn.silu(gate) * up) @ w2


def causal_depthwise_conv(x, w, b):
    k = w.shape[0]
    y = lax.conv_general_dilated(x, w[:, None, :].astype(x.dtype), window_strides=(1,), padding=[(k - 1, 0)], dimension_numbers=('NWC', 'WIO', 'NWC'), feature_group_count=x.shape[-1])
    return y + b.astype(x.dtype)


def conformer_conv(u, w_dw, b_dw, ln_g, ln_b):
    val, gate = jnp.split(u, 2, axis=-1)
    hid = causal_depthwise_conv(val * jax.nn.sigmoid(gate), w_dw, b_dw)
    return jax.nn.silu(layer_norm(hid, ln_g, ln_b))


def segsum(a):
    t = a.shape[-1]
    strict = jnp.tril(jnp.ones((t, t), dtype=bool), -1)
    incl = jnp.tril(jnp.ones((t, t), dtype=bool), 0)
    ss = jnp.cumsum(jnp.where(strict, a[..., :, None], 0.0), axis=-2)
    return jnp.where(incl, ss, -jnp.inf)


def ssd_chunked(x, a, bm, cm):
    bsz, seq, heads, hd = x.shape
    groups, nst = bm.shape[2], bm.shape[3]
    rep = heads // groups
    nc = seq // SSM_CHUNK
    x = x.reshape(bsz, nc, SSM_CHUNK, groups, rep, hd)
    a = a.reshape(bsz, nc, SSM_CHUNK, groups, rep).transpose(0, 3, 4, 1, 2)
    bm = bm.reshape(bsz, nc, SSM_CHUNK, groups, nst)
    cm = cm.reshape(bsz, nc, SSM_CHUNK, groups, nst)
    a_cum = jnp.cumsum(a, axis=-1)
    decay_in = jnp.exp(segsum(a))
    cb = jnp.einsum('bclgn,bcsgn->bgcls', cm, bm)
    y_diag = jnp.einsum('bgrcls,bcsgrp->bclgrp', cb[:, :, None] * decay_in, x)
    decay_states = jnp.exp(a_cum[..., -1:] - a_cum)
    states = jnp.einsum('bcsgn,bgrcs,bcsgrp->bcgrpn', bm, decay_states, x)
    states = jnp.concatenate([jnp.zeros_like(states[:, :1]), states], axis=1)
    chunk_tot = jnp.pad(a_cum[..., -1], ((0, 0), (0, 0), (0, 0), (1, 0)))
    decay_chunk = jnp.exp(segsum(chunk_tot))
    states = jnp.einsum('bgrzc,bcgrpn->bzgrpn', decay_chunk, states)[:, :-1]
    y_off = jnp.einsum('bclgn,bcgrpn,bgrcl->bclgrp', cm, states, jnp.exp(a_cum))
    return (y_diag + y_off).reshape(bsz, seq, heads, hd)


def mamba2_ssd(z, xbc, dt_raw, conv_w, conv_b, dt_bias, a_log, d_skip, norm_g):
    bsz, seq, _ = z.shape
    xbc = jax.nn.silu(causal_depthwise_conv(xbc, conv_w, conv_b)).astype(jnp.float32)
    xs, bm, cm = jnp.split(xbc, [SSM_D_INNER, SSM_D_INNER + SSM_GROUPS * SSM_STATE], axis=-1)
    xs = xs.reshape(bsz, seq, SSM_HEADS, SSM_HEAD_DIM)
    bm = bm.reshape(bsz, seq, SSM_GROUPS, SSM_STATE)
    cm = cm.reshape(bsz, seq, SSM_GROUPS, SSM_STATE)
    dt = jax.nn.softplus(dt_raw.astype(jnp.float32) + dt_bias.astype(jnp.float32))
    a = -jnp.exp(a_log.astype(jnp.float32))
    y = ssd_chunked(xs * dt[..., None], dt * a, bm, cm) + d_skip.astype(jnp.float32)[:, None] * xs
    y = y.reshape(bsz, seq, SSM_D_INNER) * jax.nn.silu(z.astype(jnp.float32))
    return rms_norm(y, norm_g).astype(z.dtype)


def conv_ssd_mixer(u, w_in, conv_dw_w, conv_dw_b, conv_ln_g, conv_ln_b, ssm_conv_w, ssm_conv_b, ssm_dt_bias, ssm_a_log, ssm_d, ssm_norm_g, w_out):
    proj = u @ w_in
    s0 = 2 * CONV_CH
    s1 = s0 + SSM_D_INNER
    s2 = s1 + SSM_XBC
    conv_in, z, xbc, dt_raw = jnp.split(proj, [s0, s1, s2], axis=-1)
    ya = conformer_conv(conv_in, conv_dw_w, conv_dw_b, conv_ln_g, conv_ln_b)
    yb = mamba2_ssd(z, xbc, dt_raw, ssm_conv_w, ssm_conv_b, ssm_dt_bias, ssm_a_log, ssm_d, ssm_norm_g)
    return jnp.concatenate([ya.astype(u.dtype), yb.astype(u.dtype)], axis=-1) @ w_out


def dilated_window_attention(q, k, v, slopes, window, dilation):
    bsz, seq, heads, dh = q.shape
    n = seq // dilation
    steps = window // dilation
    nb = -(-n // ATTN_BLOCK)
    pad = nb * ATTN_BLOCK - n

    def strided(t):
        t = t.reshape(bsz, n, dilation, heads, dh).transpose(0, 2, 1, 3, 4)
        t = t.reshape(bsz * dilation, n, heads, dh)
        t = jnp.pad(t, ((0, 0), (0, pad), (0, 0), (0, 0)))
        return t.reshape(bsz * dilation, nb, ATTN_BLOCK, heads, dh)

    def with_previous_block(t):
        prev = jnp.pad(t[:, :-1], ((0, 0), (1, 0), (0, 0), (0, 0), (0, 0)))
        return jnp.concatenate([prev, t], axis=2)

    qb = strided(q)
    kb = with_previous_block(strided(k))
    vb = with_previous_block(strided(v))
    scores = jnp.einsum('znqhd,znkhd->znhqk', qb.astype(jnp.float32), kb.astype(jnp.float32)) * (dh ** -0.5)
    qi = jnp.arange(ATTN_BLOCK)[:, None]
    kj = jnp.arange(2 * ATTN_BLOCK)[None, :] - ATTN_BLOCK
    rel = qi - kj
    blk0 = (jnp.arange(nb) * ATTN_BLOCK)[:, None, None]
    valid = (rel >= 0) & (rel <= steps) & (blk0 + kj >= 0)
    bias = -slopes[:, None, None] * (rel * dilation).astype(jnp.float32)
    scores = jnp.where(valid[None, :, None], scores + bias[None, None], -jnp.inf)
    lse = jax.nn.logsumexp(scores, axis=-1)
    probs = jnp.exp(scores - lse[..., None]).astype(v.dtype)
    out = jnp.einsum('znhqk,znkhd->znqhd', probs, vb)
    out = out.reshape(bsz, dilation, nb * ATTN_BLOCK, heads, dh)[:, :, :n]
    out = out.transpose(0, 2, 1, 3, 4).reshape(bsz, seq, heads, dh)
    lse = lse.transpose(0, 1, 3, 2).reshape(bsz, dilation, nb * ATTN_BLOCK, heads)[:, :, :n]
    lse = lse.transpose(0, 2, 1, 3).reshape(bsz, seq, heads)
    return out, lse


def dilated_attention_mixer(u, w_qkv, w_o):
    bsz, seq, _ = u.shape
    qkv = (u @ w_qkv).reshape(bsz, seq, 3, ATTN_GROUPS, ATTN_HEADS, ATTN_HEAD_DIM)
    slopes = jnp.exp2(-8.0 * jnp.arange(1, ATTN_HEADS + 1, dtype=jnp.float32) / ATTN_HEADS)
    outs = []
    lses = []
    for g, (window, dilation) in enumerate(DILATED_PATTERNS):
        o, l = dilated_window_attention(qkv[:, :, 0, g], qkv[:, :, 1, g], qkv[:, :, 2, g], slopes, window, dilation)
        outs.append(o)
        lses.append(l)
    weights = jax.nn.softmax(jnp.stack(lses, axis=0), axis=0)
    merged = jnp.einsum('gblh,gblhd->blhd', weights, jnp.stack(outs, axis=0).astype(jnp.float32))
    return merged.reshape(bsz, seq, ATTN_OUT_WIDTH).astype(u.dtype) @ w_o


def setup_inputs(seed: int = 0) -> dict:
    key = jax.random.key(seed)
    ks = jax.random.split(key, 20)

    def normal(k, shape, scale):
        return jax.random.normal(k, shape, jnp.float32) * scale

    x = normal(ks[0], (BATCH, SEQ, D_MODEL), 1.0)
    norm_g = 1.0 + normal(ks[1], (DEPTH, 6, D_MODEL), 0.05)
    ffn_w1 = normal(ks[2], (DEPTH, 2, D_MODEL, 2 * D_FF), D_MODEL ** -0.5)
    ffn_w2 = normal(ks[3], (DEPTH, 2, D_FF, D_MODEL), D_FF ** -0.5)
    hyb_w_in = normal(ks[4], (N_EVEN, D_MODEL, HYB_IN_COLS), D_MODEL ** -0.5)
    conv_dw_w = normal(ks[5], (N_EVEN, CONV_WIDTH, CONV_CH), CONV_WIDTH ** -0.5)
    conv_dw_b = normal(ks[6], (N_EVEN, CONV_CH), 0.02)
    conv_ln_g = 1.0 + normal(ks[7], (N_EVEN, CONV_CH), 0.05)
    conv_ln_b = normal(ks[8], (N_EVEN, CONV_CH), 0.02)
    ssm_conv_w = normal(ks[9], (N_EVEN, SSM_CONV_WIDTH, SSM_XBC), SSM_CONV_WIDTH ** -0.5)
    ssm_conv_b = normal(ks[10], (N_EVEN, SSM_XBC), 0.02)
    dt0 = jnp.exp(jax.random.uniform(ks[11], (N_EVEN, SSM_HEADS), jnp.float32, math.log(1e-3), math.log(1e-1)))
    ssm_dt_bias = dt0 + jnp.log(-jnp.expm1(-dt0))
    ssm_a_log = jnp.log(jax.random.uniform(ks[12], (N_EVEN, SSM_HEADS), jnp.float32, 1.0, 16.0))
    ssm_d = 1.0 + normal(ks[13], (N_EVEN, SSM_HEADS), 0.1)
    ssm_norm_g = 1.0 + normal(ks[14], (N_EVEN, SSM_D_INNER), 0.05)
    hyb_w_out = normal(ks[15], (N_EVEN, HYB_MIX_WIDTH, D_MODEL), HYB_MIX_WIDTH ** -0.5)
    attn_w_qkv = normal(ks[16], (N_ODD, D_MODEL, ATTN_QKV_COLS), D_MODEL ** -0.5)
    attn_w_o = normal(ks[17], (N_ODD, ATTN_OUT_WIDTH, D_MODEL), ATTN_OUT_WIDTH ** -0.5)
    return {'x': x, 'norm_g': norm_g, 'ffn_w1': ffn_w1, 'ffn_w2': ffn_w2, 'hyb_w_in': hyb_w_in, 'conv_dw_w': conv_dw_w, 'conv_dw_b': conv_dw_b, 'conv_ln_g': conv_ln_g, 'conv_ln_b': conv_ln_b, 'ssm_conv_w': ssm_conv_w, 'ssm_conv_b': ssm_conv_b, 'ssm_dt_bias': ssm_dt_bias, 'ssm_a_log': ssm_a_log, 'ssm_d': ssm_d, 'ssm_norm_g': ssm_norm_g, 'hyb_w_out': hyb_w_out, 'attn_w_qkv': attn_w_qkv, 'attn_w_o': attn_w_o}


def reference(x, norm_g, ffn_w1, ffn_w2, hyb_w_in, conv_dw_w, conv_dw_b, conv_ln_g, conv_ln_b, ssm_conv_w, ssm_conv_b, ssm_dt_bias, ssm_a_log, ssm_d, ssm_norm_g, hyb_w_out, attn_w_qkv, attn_w_o):
    h = x
    for i in range(DEPTH):
        g = norm_g[i]
        h = h + 0.5 * rms_norm(swiglu(rms_norm(h, g[0]), ffn_w1[i, 0], ffn_w2[i, 0]), g[1])
        u = rms_norm(h, g[2])
        j = i // 2
        if i % 2 == 0:
            m = conv_ssd_mixer(u, hyb_w_in[j], conv_dw_w[j], conv_dw_b[j], conv_ln_g[j], conv_ln_b[j], ssm_conv_w[j], ssm_conv_b[j], ssm_dt_bias[j], ssm_a_log[j], ssm_d[j], ssm_norm_g[j], hyb_w_out[j])
        else:
            m = dilated_attention_mixer(u, attn_w_qkv[j], attn_w_o[j])
        h = h + rms_norm(m, g[3])
        h = h + 0.5 * rms_norm(swiglu(rms_norm(h, g[4]), ffn_w1[i, 1], ffn_w2[i, 1]), g[5])
    return h
```

```python
import functools

import jax
import jax.numpy as jnp
from jax import lax
from jax.experimental import pallas as pl
from jax.experimental.pallas import tpu as pltpu

NORM_EPS = 1e-6
CONV_WIDTH = 31
SSM_HEAD_DIM = 64
SSM_HEADS = 16
SSM_GROUPS = 2
SSM_STATE = 128
SSM_CONV_WIDTH = 4
SSM_CHUNK = 128
ATTN_HEAD_DIM = 64
ATTN_HEADS = 8
DILATED_PATTERNS = ((128, 1), (512, 4), (2048, 16))
ATTN_BLOCK = 128

LANES = 128
SUBLANES = 8
VMEM_LIMIT_BYTES = 56 * 1024 * 1024
MASK_VALUE = -1e30

F32 = jnp.float32
BF16 = jnp.bfloat16
HIGHEST = lax.Precision.HIGHEST


def _params(*sem):
    return pltpu.CompilerParams(dimension_semantics=sem, vmem_limit_bytes=VMEM_LIMIT_BYTES)


def _resident(shape):
    nd = len(shape)
    return pl.BlockSpec(shape, lambda *_: (0,) * nd, pipeline_mode=pl.Buffered(1))


def _rms(x, g):
    return x * lax.rsqrt(jnp.mean(x * x, axis=-1, keepdims=True) + NORM_EPS) * g


def _silu(x):
    return x * jax.nn.sigmoid(x)


def _mm(a, b):
    return jnp.dot(a, b, preferred_element_type=F32)


def _ffn_kernel(h_ref, gpre_ref, gpost_ref, wg_ref, wu_ref, w2_ref, o_ref):
    h = h_ref[...]
    u = _rms(h, gpre_ref[...]).astype(BF16)
    gate = _mm(u, wg_ref[...])
    up = _mm(u, wu_ref[...])
    act = (_silu(gate) * up).astype(BF16)
    y = _mm(act, w2_ref[...])
    o_ref[...] = h + 0.5 * _rms(y, gpost_ref[...])


def _ffn(h, g_pre, g_post, w1, w2, tm=512):
    t, d = h.shape
    f = w2.shape[0]
    wg = w1[:, :f].astype(BF16)
    wu = w1[:, f:].astype(BF16)
    row = pl.BlockSpec((tm, d), lambda i: (i, 0))
    return pl.pallas_call(
        _ffn_kernel,
        out_shape=jax.ShapeDtypeStruct((t, d), F32),
        grid=(t // tm,),
        in_specs=[row, _resident((1, d)), _resident((1, d)),
                  _resident((d, f)), _resident((d, f)), _resident((f, d))],
        out_specs=row,
        compiler_params=_params("parallel"),
        name="ffn",
    )(h, g_pre.reshape(1, d), g_post.reshape(1, d), wg, wu, w2.astype(BF16))


def _rms_matmul_kernel(h_ref, g_ref, w_ref, *o_refs):
    u = _rms(h_ref[...], g_ref[...]).astype(BF16)
    p = _mm(u, w_ref[...])
    start = 0
    for o_ref in o_refs:
        width = o_ref.shape[-1]
        o_ref[...] = p[:, start:start + width].astype(o_ref.dtype)
        start += width


def _rms_matmul(h, g, w, widths, dtype, tm=512):
    t, d = h.shape
    n = w.shape[1]
    assert sum(widths) == n
    return pl.pallas_call(
        _rms_matmul_kernel,
        out_shape=[jax.ShapeDtypeStruct((t, wd), dtype) for wd in widths],
        grid=(t // tm,),
        in_specs=[pl.BlockSpec((tm, d), lambda i: (i, 0)), _resident((1, d)), _resident((d, n))],
        out_specs=[pl.BlockSpec((tm, wd), lambda i: (i, 0)) for wd in widths],
        compiler_params=_params("parallel"),
        name="rms_matmul",
    )(h, g.reshape(1, d), w)


CONV_TAIL = 32
CONV_ROW_BLOCK = 64


def _convmod_kernel(val_ref, gate_ref, w_ref, b_ref, lg_ref, lb_ref, o_ref, buf_ref, hid_ref, *, tl):
    ch = val_ref.shape[-1]

    @pl.when(pl.program_id(1) == 0)
    def _():
        buf_ref[0:CONV_TAIL, :] = jnp.zeros((CONV_TAIL, ch), F32)

    @pl.when(pl.program_id(1) > 0)
    def _():
        buf_ref[0:CONV_TAIL, :] = buf_ref[tl:tl + CONV_TAIL, :]

    buf_ref[CONV_TAIL:CONV_TAIL + tl, :] = val_ref[...] * jax.nn.sigmoid(gate_ref[...])

    first = CONV_TAIL - (CONV_WIDTH - 1)
    for c in range(ch // LANES):
        cols = slice(c * LANES, (c + 1) * LANES)
        wc = w_ref[:, cols]
        bc = b_ref[:, cols]
        for rb in range(tl // CONV_ROW_BLOCK):
            r0 = rb * CONV_ROW_BLOCK
            acc = jnp.broadcast_to(bc, (CONV_ROW_BLOCK, LANES))
            for k in range(CONV_WIDTH):
                acc = acc + buf_ref[r0 + first + k:r0 + first + k + CONV_ROW_BLOCK, cols] * wc[k:k + 1, :]
            hid_ref[r0:r0 + CONV_ROW_BLOCK, cols] = acc

    hid = hid_ref[...]
    xc = hid - jnp.mean(hid, axis=-1, keepdims=True)
    y = xc * lax.rsqrt(jnp.mean(xc * xc, axis=-1, keepdims=True) + NORM_EPS)
    o_ref[...] = _silu(y * lg_ref[...] + lb_ref[...]).astype(o_ref.dtype)


def _convmod(proj, w, b, ln_g, ln_b, tl=256):
    bsz, seq, _ = proj.shape
    ch = w.shape[1]
    wpad = jnp.zeros((CONV_TAIL, ch), F32).at[:CONV_WIDTH].set(w)
    vec = _resident((1, ch))
    return pl.pallas_call(
        functools.partial(_convmod_kernel, tl=tl),
        out_shape=jax.ShapeDtypeStruct((bsz, seq, ch), BF16),
        grid=(bsz, seq // tl),
        in_specs=[pl.BlockSpec((None, tl, ch), lambda bi, i: (bi, i, 0)),
                  pl.BlockSpec((None, tl, ch), lambda bi, i: (bi, i, 1)),
                  _resident((CONV_TAIL, ch)), vec, vec, vec],
        out_specs=pl.BlockSpec((None, tl, ch), lambda bi, i: (bi, i, 0)),
        scratch_shapes=[pltpu.VMEM((tl + CONV_TAIL, ch), F32), pltpu.VMEM((tl, ch), F32)],
        compiler_params=_params("parallel", "arbitrary"),
        name="convmod",
    )(proj, proj, wpad, b.reshape(1, ch), ln_g.reshape(1, ch), ln_b.reshape(1, ch))


def _ssd_kernel(xbc_ref, z_ref, dt_ref, cw_ref, cb_ref, dtb_ref, alog_ref, dskip_ref, ng_ref,
                expand_ref, tri_ref, o_ref, xbuf_ref, state_ref):
    q = SSM_CHUNK
    d_inner = SSM_HEADS * SSM_HEAD_DIM
    gcols = d_inner // SSM_GROUPS
    nst = SSM_STATE

    @pl.when(pl.program_id(1) == 0)
    def _():
        xbuf_ref[0:SUBLANES, :] = jnp.zeros((SUBLANES, xbuf_ref.shape[1]), F32)
        state_ref[...] = jnp.zeros(state_ref.shape, F32)

    @pl.when(pl.program_id(1) > 0)
    def _():
        xbuf_ref[0:SUBLANES, :] = xbuf_ref[q:q + SUBLANES, :]

    xbuf_ref[SUBLANES:SUBLANES + q, :] = xbc_ref[...]
    first = SUBLANES - (SSM_CONV_WIDTH - 1)
    acc = jnp.broadcast_to(cb_ref[...], (q, xbuf_ref.shape[1]))
    for k in range(SSM_CONV_WIDTH):
        acc = acc + xbuf_ref[first + k:first + k + q, :] * cw_ref[k:k + 1, :]
    xbc = _silu(acc)
    xs = xbc[:, :d_inner]
    bm = xbc[:, d_inner:d_inner + SSM_GROUPS * nst]
    cm = xbc[:, d_inner + SSM_GROUPS * nst:]

    dt = jax.nn.softplus(dt_ref[...] + dtb_ref[...])
    a = dt * (-jnp.exp(alog_ref[...]))
    acum = jnp.dot(tri_ref[...], a, precision=HIGHEST, preferred_element_type=F32)
    atot = acum[q - 1:q, :]
    per_head = jnp.concatenate([dt, jnp.exp(acum), jnp.exp(atot - acum)], axis=0)
    full = jnp.dot(per_head, expand_ref[...], precision=HIGHEST, preferred_element_type=F32)
    dt_full = full[0:q]
    eacum_full = full[q:2 * q]
    dstate_full = full[2 * q:3 * q]

    xdt = xs * dt_full
    xw = (xdt * dstate_full).astype(BF16)
    acum_t = acum.T
    rows = lax.broadcasted_iota(jnp.int32, (q, q), 0)
    cols = lax.broadcasted_iota(jnp.int32, (q, q), 1)
    causal = rows >= cols
    lane = lax.broadcasted_iota(jnp.int32, (q, LANES), 1)
    low_half = lane < SSM_HEAD_DIM

    state = state_ref[...]
    y_diag = []
    y_off = []
    new_states = []
    heads_per_group = SSM_HEADS // SSM_GROUPS
    for g in range(SSM_GROUPS):
        bg = bm[:, g * nst:(g + 1) * nst].astype(BF16)
        cg = cm[:, g * nst:(g + 1) * nst].astype(BF16)
        gsl = slice(g * gcols, (g + 1) * gcols)
        cb = lax.dot_general(cg, bg, (((1,), (1,)), ((), ())), preferred_element_type=F32)
        y_off.append(_mm(cg, state[:, gsl].astype(BF16)))
        new_states.append(lax.dot_general(bg, xw[:, gsl], (((0,), (0,)), ((), ())), preferred_element_type=F32))
        for pair in range(heads_per_group // 2):
            h0 = g * heads_per_group + 2 * pair
            masks = []
            for h in (h0, h0 + 1):
                diff = acum[:, h:h + 1] - acum_t[h:h + 1, :]
                masks.append((cb * jnp.exp(jnp.where(causal, diff, MASK_VALUE))).astype(BF16))
            lhs = jnp.concatenate(masks, axis=1)
            xp = xdt[:, h0 * SSM_HEAD_DIM:(h0 + 2) * SSM_HEAD_DIM]
            rhs = jnp.concatenate([jnp.where(low_half, xp, 0.0), jnp.where(low_half, 0.0, xp)], axis=0).astype(BF16)
            y_diag.append(_mm(lhs, rhs))
    y = (jnp.concatenate(y_diag, axis=1) + jnp.concatenate(y_off, axis=1) * eacum_full
         + dskip_ref[...] * xs)
    state_ref[...] = state * eacum_full[q - 1:q, :] + jnp.concatenate(new_states, axis=1)
    y = y * _silu(z_ref[...])
    o_ref[...] = _rms(y, ng_ref[...]).astype(o_ref.dtype)


def _ssd(proj, dt_raw, conv_w, conv_b, dt_bias, a_log, d_skip, norm_g, z_col, xbc_col):
    bsz, seq, _ = proj.shape
    q = SSM_CHUNK
    d_inner = SSM_HEADS * SSM_HEAD_DIM
    xbc_w = conv_w.shape[1]
    cw = jnp.zeros((SUBLANES, xbc_w), F32).at[:SSM_CONV_WIDTH].set(conv_w)
    pad = LANES - SSM_HEADS
    head_of_col = jnp.arange(d_inner) // SSM_HEAD_DIM
    expand = (jnp.arange(LANES)[:, None] == head_of_col[None, :]).astype(F32)
    tri = (jnp.arange(q)[:, None] >= jnp.arange(q)[None, :]).astype(F32)
    return pl.pallas_call(
        _ssd_kernel,
        out_shape=jax.ShapeDtypeStruct((bsz, seq, d_inner), BF16),
        grid=(bsz, seq // q),
        in_specs=[pl.BlockSpec((None, q, xbc_w), lambda bi, i: (bi, i, xbc_col)),
                  pl.BlockSpec((None, q, d_inner), lambda bi, i: (bi, i, z_col)),
                  pl.BlockSpec((None, q, LANES), lambda bi, i: (bi, i, 0)),
                  _resident((SUBLANES, xbc_w)), _resident((1, xbc_w)),
                  _resident((1, LANES)), _resident((1, LANES)),
                  _resident((1, d_inner)), _resident((1, d_inner)),
                  _resident((LANES, d_inner)), _resident((q, q))],
        out_specs=pl.BlockSpec((None, q, d_inner), lambda bi, i: (bi, i, 0)),
        scratch_shapes=[pltpu.VMEM((q + SUBLANES, xbc_w), F32), pltpu.VMEM((SSM_STATE, d_inner), F32)],
        compiler_params=_params("parallel", "arbitrary"),
        name="ssd",
    )(proj, proj, dt_raw, cw, conv_b.reshape(1, xbc_w),
      jnp.pad(dt_bias, (0, pad)).reshape(1, LANES), jnp.pad(a_log, (0, pad)).reshape(1, LANES),
      jnp.repeat(d_skip, SSM_HEAD_DIM).reshape(1, d_inner), norm_g.reshape(1, d_inner), expand, tri)


def _proj_res_kernel(h_ref, xa_ref, xb_ref, wa_ref, wb_ref, g_ref, o_ref):
    y = _mm(xa_ref[...], wa_ref[...]) + _mm(xb_ref[...], wb_ref[...])
    o_ref[...] = h_ref[...] + _rms(y, g_ref[...])


def _proj_res(h, xa, xb, w, g, tm=512):
    t, d = h.shape
    ka, kb = xa.shape[1], xb.shape[1]
    w = w.astype(BF16)
    row = lambda n: pl.BlockSpec((tm, n), lambda i: (i, 0))
    return pl.pallas_call(
        _proj_res_kernel,
        out_shape=jax.ShapeDtypeStruct((t, d), F32),
        grid=(t // tm,),
        in_specs=[row(d), row(ka), row(kb), _resident((ka, d)), _resident((kb, d)), _resident((1, d))],
        out_specs=row(d),
        compiler_params=_params("parallel"),
        name="proj_res",
    )(h, xa, xb, w[:ka], w[ka:], g.reshape(1, d))


def _attn_kernel(q_ref, kp_ref, kc_ref, vp_ref, vc_ref, bias_ref, o_ref, lse_ref):
    dh = ATTN_HEAD_DIM
    q = q_ref[...] * (dh ** -0.5)
    k = jnp.concatenate([kp_ref[...], kc_ref[...]], axis=0)
    v = jnp.concatenate([vp_ref[...], vc_ref[...]], axis=0)
    for h in range(ATTN_HEADS):
        hs = slice(h * dh, (h + 1) * dh)
        s = lax.dot_general(q[:, hs], k[:, hs], (((1,), (1,)), ((), ())), preferred_element_type=F32)
        s = s + bias_ref[h]
        m = jnp.max(s, axis=-1, keepdims=True)
        p = jnp.exp(s - m)
        l = jnp.sum(p, axis=-1, keepdims=True)
        o = _mm(p.astype(BF16), v[:, hs]) / l
        o_ref[:, hs] = o
        lse_ref[:, hs] = jnp.broadcast_to(m + jnp.log(l), (ATTN_BLOCK, dh))


def _attn_bias(window, dilation):
    steps = window // dilation
    slopes = jnp.exp2(-8.0 * jnp.arange(1, ATTN_HEADS + 1, dtype=F32) / ATTN_HEADS)
    qi = jnp.arange(ATTN_BLOCK)[:, None]
    kj = jnp.arange(2 * ATTN_BLOCK)[None, :] - ATTN_BLOCK
    rel = qi - kj
    valid = (rel >= 0) & (rel <= steps)
    bias = -slopes[:, None, None] * (rel * dilation).astype(F32)
    later = jnp.where(valid[None], bias, MASK_VALUE)
    first = jnp.where((valid & (kj >= 0))[None], bias, MASK_VALUE)
    return jnp.stack([first, later], axis=0)


def _attn_group(qkv, g, window, dilation, bsz, seq):
    ngroups = len(DILATED_PATTERNS)
    width = ATTN_HEADS * ATTN_HEAD_DIM
    nblk_cols = 3 * ngroups
    n = seq // dilation
    nb = n // ATTN_BLOCK
    assert window // dilation <= ATTN_BLOCK and nb * ATTN_BLOCK == n
    view = qkv.reshape(bsz, n, dilation * nblk_cols * width)
    blk = (None, ATTN_BLOCK, width)

    def cur(which):
        return pl.BlockSpec(blk, lambda b, r, j: (b, j, r * nblk_cols + which * ngroups + g))

    def prev(which):
        return pl.BlockSpec(blk, lambda b, r, j: (b, jnp.maximum(j - 1, 0), r * nblk_cols + which * ngroups + g))

    out_spec = pl.BlockSpec(blk, lambda b, r, j: (b, j, r))
    out_shape = jax.ShapeDtypeStruct((bsz, n, dilation * width), F32)
    o, lse = pl.pallas_call(
        _attn_kernel,
        out_shape=[out_shape, out_shape],
        grid=(bsz, dilation, nb),
        in_specs=[cur(0), prev(1), cur(1), prev(2), cur(2),
                  pl.BlockSpec((None, ATTN_HEADS, ATTN_BLOCK, 2 * ATTN_BLOCK),
                               lambda b, r, j: (jnp.minimum(j, 1), 0, 0, 0))],
        out_specs=[out_spec, out_spec],
        compiler_params=_params("parallel", "parallel", "arbitrary"),
        name=f"attn_d{dilation}",
    )(view, view, view, view, view, _attn_bias(window, dilation))
    return o.reshape(bsz * seq, width), lse.reshape(bsz * seq, width)


def _merge_proj_kernel(h_ref, o0_ref, o1_ref, o2_ref, l0_ref, l1_ref, l2_ref, w_ref, g_ref, out_ref):
    l0, l1, l2 = l0_ref[...], l1_ref[...], l2_ref[...]
    m = jnp.maximum(jnp.maximum(l0, l1), l2)
    e0, e1, e2 = jnp.exp(l0 - m), jnp.exp(l1 - m), jnp.exp(l2 - m)
    merged = (e0 * o0_ref[...] + e1 * o1_ref[...] + e2 * o2_ref[...]) / (e0 + e1 + e2)
    y = _mm(merged.astype(BF16), w_ref[...])
    out_ref[...] = h_ref[...] + _rms(y, g_ref[...])


def _merge_proj(h, outs, lses, w, g, tm=512):
    t, d = h.shape
    k = w.shape[0]
    row = lambda n: pl.BlockSpec((tm, n), lambda i: (i, 0))
    return pl.pallas_call(
        _merge_proj_kernel,
        out_shape=jax.ShapeDtypeStruct((t, d), F32),
        grid=(t // tm,),
        in_specs=[row(d)] + [row(k)] * 6 + [_resident((k, d)), _resident((1, d))],
        out_specs=row(d),
        compiler_params=_params("parallel"),
        name="merge_proj",
    )(h, *outs, *lses, w.astype(BF16), g.reshape(1, d))


def _conv_ssd_mixer(h, bsz, seq, g_pre, g_post, w_in, conv_dw_w, conv_dw_b, conv_ln_g, conv_ln_b,
                    ssm_conv_w, ssm_conv_b, ssm_dt_bias, ssm_a_log, ssm_d, ssm_norm_g, w_out):
    conv_ch = conv_dw_w.shape[1]
    d_inner = SSM_HEADS * SSM_HEAD_DIM
    xbc_w = ssm_conv_w.shape[1]
    main = 2 * conv_ch + d_inner + xbc_w
    w = jnp.pad(w_in, ((0, 0), (0, LANES - SSM_HEADS))).astype(BF16)
    proj, dt_raw = _rms_matmul(h, g_pre, w, (main, LANES), F32)
    proj = proj.reshape(bsz, seq, main)
    dt_raw = dt_raw.reshape(bsz, seq, LANES)
    assert (2 * conv_ch) % d_inner == 0 and (2 * conv_ch + d_inner) % xbc_w == 0
    ya = _convmod(proj, conv_dw_w, conv_dw_b, conv_ln_g, conv_ln_b)
    yb = _ssd(proj, dt_raw, ssm_conv_w, ssm_conv_b, ssm_dt_bias, ssm_a_log, ssm_d, ssm_norm_g,
              z_col=(2 * conv_ch) // d_inner, xbc_col=(2 * conv_ch + d_inner) // xbc_w)
    t = bsz * seq
    return _proj_res(h, ya.reshape(t, conv_ch), yb.reshape(t, d_inner), w_out, g_post)


def _attention_mixer(h, bsz, seq, g_pre, g_post, w_qkv, w_o):
    n = w_qkv.shape[1]
    (qkv,) = _rms_matmul(h, g_pre, w_qkv.astype(BF16), (n,), BF16)
    outs, lses = [], []
    for g, (window, dilation) in enumerate(DILATED_PATTERNS):
        o, lse = _attn_group(qkv, g, window, dilation, bsz, seq)
        outs.append(o)
        lses.append(lse)
    return _merge_proj(h, outs, lses, w_o, g_post)


def kernel(x, norm_g, ffn_w1, ffn_w2, hyb_w_in, conv_dw_w, conv_dw_b, conv_ln_g, conv_ln_b, ssm_conv_w, ssm_conv_b, ssm_dt_bias, ssm_a_log, ssm_d, ssm_norm_g, hyb_w_out, attn_w_qkv, attn_w_o):
    bsz, seq, d = x.shape
    h = x.reshape(bsz * seq, d)
    for i in range(norm_g.shape[0]):
        g = norm_g[i]
        j = i // 2
        h = _ffn(h, g[0], g[1], ffn_w1[i, 0], ffn_w2[i, 0])
        if i % 2 == 0:
            h = _conv_ssd_mixer(h, bsz, seq, g[2], g[3], hyb_w_in[j], conv_dw_w[j], conv_dw_b[j],
                                conv_ln_g[j], conv_ln_b[j], ssm_conv_w[j], ssm_conv_b[j], ssm_dt_bias[j],
                                ssm_a_log[j], ssm_d[j], ssm_norm_g[j], hyb_w_out[j])
        else:
            h = _attention_mixer(h, bsz, seq, g[2], g[3], attn_w_qkv[j], attn_w_o[j])
        h = _ffn(h, g[4], g[5], ffn_w1[i, 1], ffn_w2[i, 1])
    return h.reshape(bsz, seq, d)
```

```python
import functools

import jax
import jax.numpy as jnp
from jax import lax
from jax.experimental import pallas as pl
from jax.experimental.pallas import tpu as pltpu

NORM_EPS = 1e-6
CONV_WIDTH = 31
SSM_HEAD_DIM = 64
SSM_HEADS = 16
SSM_GROUPS = 2
SSM_STATE = 128
SSM_CONV_WIDTH = 4
SSM_CHUNK = 128
ATTN_HEAD_DIM = 64
ATTN_HEADS = 8
DILATED_PATTERNS = ((128, 1), (512, 4), (2048, 16))
ATTN_BLOCK = 128

LANES = 128
SUBLANES = 8
VMEM_LIMIT_BYTES = 56 * 1024 * 1024
MASK_VALUE = -1e30

F32 = jnp.float32
BF16 = jnp.bfloat16


def _params(*sem):
    return pltpu.CompilerParams(dimension_semantics=sem, vmem_limit_bytes=VMEM_LIMIT_BYTES)


def _resident(shape):
    nd = len(shape)
    return pl.BlockSpec(shape, lambda *_: (0,) * nd, pipeline_mode=pl.Buffered(1))


def _rms(x, g):
    return x * lax.rsqrt(jnp.mean(x * x, axis=-1, keepdims=True) + NORM_EPS) * g


def _silu(x):
    return x * jax.nn.sigmoid(x)


def _mm(a, b):
    return jnp.dot(a, b, preferred_element_type=F32)


def _ffn_kernel(h_ref, gpre_ref, gpost_ref, wg_ref, wu_ref, w2_ref, o_ref):
    h = h_ref[...]
    u = _rms(h, gpre_ref[...]).astype(BF16)
    gate = _mm(u, wg_ref[...])
    up = _mm(u, wu_ref[...])
    act = (_silu(gate) * up).astype(BF16)
    y = _mm(act, w2_ref[...])
    o_ref[...] = h + 0.5 * _rms(y, gpost_ref[...])


def _ffn(h, g_pre, g_post, w1, w2, tm=512):
    t, d = h.shape
    f = w2.shape[0]
    wg = w1[:, :f].astype(BF16)
    wu = w1[:, f:].astype(BF16)
    row = pl.BlockSpec((tm, d), lambda i: (i, 0))
    return pl.pallas_call(
        _ffn_kernel,
        out_shape=jax.ShapeDtypeStruct((t, d), F32),
        grid=(t // tm,),
        in_specs=[row, _resident((1, d)), _resident((1, d)),
                  _resident((d, f)), _resident((d, f)), _resident((f, d))],
        out_specs=row,
        compiler_params=_params("parallel"),
        name="ffn",
    )(h, g_pre.reshape(1, d), g_post.reshape(1, d), wg, wu, w2.astype(BF16))


def _rms_matmul_kernel(h_ref, g_ref, w_ref, *o_refs):
    u = _rms(h_ref[...], g_ref[...]).astype(BF16)
    p = _mm(u, w_ref[...])
    start = 0
    for o_ref in o_refs:
        width = o_ref.shape[-1]
        o_ref[...] = p[:, start:start + width].astype(o_ref.dtype)
        start += width


def _rms_matmul(h, g, w, widths, dtype, tm=512):
    t, d = h.shape
    n = w.shape[1]
    assert sum(widths) == n
    return pl.pallas_call(
        _rms_matmul_kernel,
        out_shape=[jax.ShapeDtypeStruct((t, wd), dtype) for wd in widths],
        grid=(t // tm,),
        in_specs=[pl.BlockSpec((tm, d), lambda i: (i, 0)), _resident((1, d)), _resident((d, n))],
        out_specs=[pl.BlockSpec((tm, wd), lambda i: (i, 0)) for wd in widths],
        compiler_params=_params("parallel"),
        name="rms_matmul",
    )(h, g.reshape(1, d), w)


CONV_TAIL = 32
CONV_ROW_SPLIT = 2


def _convmod_kernel(val_ref, gate_ref, w_ref, b_ref, lg_ref, lb_ref, o_ref, buf_ref, hid_ref, *, tl):
    ch = val_ref.shape[-1]
    nt = ch // LANES

    @pl.when(pl.program_id(1) == 0)
    def _():
        buf_ref[0, :, 0:CONV_TAIL, :] = jnp.zeros((nt, CONV_TAIL, LANES), F32)

    @pl.when(pl.program_id(1) > 0)
    def _():
        buf_ref[0, :, 0:CONV_TAIL, :] = buf_ref[0, :, tl:tl + CONV_TAIL, :]

    glu = val_ref[...] * jax.nn.sigmoid(gate_ref[...])
    for c in range(nt):
        buf_ref[0, c, CONV_TAIL:CONV_TAIL + tl, :] = glu[:, c * LANES:(c + 1) * LANES]

    first = CONV_TAIL - (CONV_WIDTH - 1)
    shifted_rows = tl + CONV_TAIL - SUBLANES
    rows = tl // CONV_ROW_SPLIT
    tiles = (rows // SUBLANES, SUBLANES, LANES)

    def lane_tile(c, carry):
        for r in range(1, SUBLANES):
            buf_ref[r, c, 0:shifted_rows, :] = buf_ref[0, c, r:r + shifted_rows, :]
        wc = w_ref[c]
        wk = [jnp.broadcast_to(wc[k:k + 1, :], (SUBLANES, LANES))[None] for k in range(CONV_WIDTH)]
        for part in range(CONV_ROW_SPLIT):
            acc = jnp.broadcast_to(b_ref[c], tiles)
            for k in range(CONV_WIDTH):
                shift = (first + k) % SUBLANES
                row = part * rows + first + k - shift
                acc = acc + buf_ref[shift, c, row:row + rows, :].reshape(tiles) * wk[k]
            hid_ref[c, part * rows:(part + 1) * rows, :] = acc.reshape(rows, LANES)
        return carry

    lax.fori_loop(0, nt, lane_tile, 0)

    hid = jnp.concatenate([hid_ref[c] for c in range(nt)], axis=1)
    xc = hid - jnp.mean(hid, axis=-1, keepdims=True)
    y = xc * lax.rsqrt(jnp.mean(xc * xc, axis=-1, keepdims=True) + NORM_EPS)
    o_ref[...] = _silu(y * lg_ref[...] + lb_ref[...]).astype(o_ref.dtype)


def _convmod(proj, w, b, ln_g, ln_b, tl=256):
    bsz, seq, _ = proj.shape
    ch = w.shape[1]
    nt = ch // LANES
    wpad = jnp.zeros((CONV_TAIL, ch), F32).at[:CONV_WIDTH].set(w)
    wpad = wpad.reshape(CONV_TAIL, nt, LANES).transpose(1, 0, 2)
    vec = _resident((1, ch))
    return pl.pallas_call(
        functools.partial(_convmod_kernel, tl=tl),
        out_shape=jax.ShapeDtypeStruct((bsz, seq, ch), BF16),
        grid=(bsz, seq // tl),
        in_specs=[pl.BlockSpec((None, tl, ch), lambda bi, i: (bi, i, 0)),
                  pl.BlockSpec((None, tl, ch), lambda bi, i: (bi, i, 1)),
                  _resident((nt, CONV_TAIL, LANES)), _resident((nt, 1, LANES)), vec, vec],
        out_specs=pl.BlockSpec((None, tl, ch), lambda bi, i: (bi, i, 0)),
        scratch_shapes=[pltpu.VMEM((SUBLANES, nt, tl + CONV_TAIL, LANES), F32), pltpu.VMEM((nt, tl, LANES), F32)],
        compiler_params=_params("parallel", "arbitrary"),
        name="convmod",
    )(proj, proj, wpad, b.reshape(nt, 1, LANES), ln_g.reshape(1, ch), ln_b.reshape(1, ch))


def _ssd_kernel(xbc_ref, z_ref, dt_ref, cw_ref, cb_ref, dtb_ref, alog_ref, dskip_ref, ng_ref,
                expand_ref, tri_ref, o_ref, xbuf_ref, state_ref):
    q = SSM_CHUNK
    d_inner = SSM_HEADS * SSM_HEAD_DIM
    gcols = d_inner // SSM_GROUPS
    nst = SSM_STATE

    @pl.when(pl.program_id(1) == 0)
    def _():
        xbuf_ref[0:SUBLANES, :] = jnp.zeros((SUBLANES, xbuf_ref.shape[1]), F32)
        state_ref[...] = jnp.zeros(state_ref.shape, F32)

    @pl.when(pl.program_id(1) > 0)
    def _():
        xbuf_ref[0:SUBLANES, :] = xbuf_ref[q:q + SUBLANES, :]

    xbuf_ref[SUBLANES:SUBLANES + q, :] = xbc_ref[...]
    first = SUBLANES - (SSM_CONV_WIDTH - 1)
    acc = jnp.broadcast_to(cb_ref[...], (q, xbuf_ref.shape[1]))
    for k in range(SSM_CONV_WIDTH):
        acc = acc + xbuf_ref[first + k:first + k + q, :] * cw_ref[k:k + 1, :]
    xbc = _silu(acc)
    xs = xbc[:, :d_inner]
    bm = xbc[:, d_inner:d_inner + SSM_GROUPS * nst]
    cm = xbc[:, d_inner + SSM_GROUPS * nst:]

    dt = jax.nn.softplus(dt_ref[...] + dtb_ref[...])
    a = dt * (-jnp.exp(alog_ref[...]))
    tri = tri_ref[...]
    a_hi = a.astype(BF16)
    a_mid = (a - a_hi.astype(F32)).astype(BF16)
    a_lo = (a - a_hi.astype(F32) - a_mid.astype(F32)).astype(BF16)
    acum = _mm(tri, a_hi) + _mm(tri, a_mid) + _mm(tri, a_lo)
    atot = acum[q - 1:q, :]
    per_head = jnp.concatenate([dt, jnp.exp(acum), jnp.exp(atot - acum)], axis=0)
    ph_hi = per_head.astype(BF16)
    ph_lo = (per_head - ph_hi.astype(F32)).astype(BF16)
    full = _mm(ph_hi, expand_ref[...]) + _mm(ph_lo, expand_ref[...])
    dt_full = full[0:q]
    eacum_full = full[q:2 * q]
    dstate_full = full[2 * q:3 * q]

    xdt = xs * dt_full
    xw = (xdt * dstate_full).astype(BF16)
    acum_t = acum.T
    rows = lax.broadcasted_iota(jnp.int32, (q, q), 0)
    cols = lax.broadcasted_iota(jnp.int32, (q, q), 1)
    causal = rows >= cols
    lane = lax.broadcasted_iota(jnp.int32, (q, LANES), 1)
    low_half = lane < SSM_HEAD_DIM

    state = state_ref[...]
    y_diag = []
    y_off = []
    new_states = []
    heads_per_group = SSM_HEADS // SSM_GROUPS
    for g in range(SSM_GROUPS):
        bg = bm[:, g * nst:(g + 1) * nst].astype(BF16)
        cg = cm[:, g * nst:(g + 1) * nst].astype(BF16)
        gsl = slice(g * gcols, (g + 1) * gcols)
        cb = lax.dot_general(cg, bg, (((1,), (1,)), ((), ())), preferred_element_type=F32)
        y_off.append(_mm(cg, state[:, gsl].astype(BF16)))
        new_states.append(lax.dot_general(bg, xw[:, gsl], (((0,), (0,)), ((), ())), preferred_element_type=F32))
        for pair in range(heads_per_group // 2):
            h0 = g * heads_per_group + 2 * pair
            masks = []
            for h in (h0, h0 + 1):
                diff = acum[:, h:h + 1] - acum_t[h:h + 1, :]
                masks.append((cb * jnp.exp(jnp.where(causal, diff, MASK_VALUE))).astype(BF16))
            lhs = jnp.concatenate(masks, axis=1)
            xp = xdt[:, h0 * SSM_HEAD_DIM:(h0 + 2) * SSM_HEAD_DIM]
            rhs = jnp.concatenate([jnp.where(low_half, xp, 0.0), jnp.where(low_half, 0.0, xp)], axis=0).astype(BF16)
            y_diag.append(_mm(lhs, rhs))
    y = (jnp.concatenate(y_diag, axis=1) + jnp.concatenate(y_off, axis=1) * eacum_full
         + dskip_ref[...] * xs)
    state_ref[...] = state * eacum_full[q - 1:q, :] + jnp.concatenate(new_states, axis=1)
    y = y * _silu(z_ref[...])
    o_ref[...] = _rms(y, ng_ref[...]).astype(o_ref.dtype)


def _ssd(proj, dt_raw, conv_w, conv_b, dt_bias, a_log, d_skip, norm_g, z_col, xbc_col):
    bsz, seq, _ = proj.shape
    q = SSM_CHUNK
    d_inner = SSM_HEADS * SSM_HEAD_DIM
    xbc_w = conv_w.shape[1]
    cw = jnp.zeros((SUBLANES, xbc_w), F32).at[:SSM_CONV_WIDTH].set(conv_w)
    pad = LANES - SSM_HEADS
    head_of_col = jnp.arange(d_inner) // SSM_HEAD_DIM
    expand = (jnp.arange(LANES)[:, None] == head_of_col[None, :]).astype(BF16)
    tri = (jnp.arange(q)[:, None] >= jnp.arange(q)[None, :]).astype(BF16)
    return pl.pallas_call(
        _ssd_kernel,
        out_shape=jax.ShapeDtypeStruct((bsz, seq, d_inner), BF16),
        grid=(bsz, seq // q),
        in_specs=[pl.BlockSpec((None, q, xbc_w), lambda bi, i: (bi, i, xbc_col)),
                  pl.BlockSpec((None, q, d_inner), lambda bi, i: (bi, i, z_col)),
                  pl.BlockSpec((None, q, LANES), lambda bi, i: (bi, i, 0)),
                  _resident((SUBLANES, xbc_w)), _resident((1, xbc_w)),
                  _resident((1, LANES)), _resident((1, LANES)),
                  _resident((1, d_inner)), _resident((1, d_inner)),
                  _resident((LANES, d_inner)), _resident((q, q))],
        out_specs=pl.BlockSpec((None, q, d_inner), lambda bi, i: (bi, i, 0)),
        scratch_shapes=[pltpu.VMEM((q + SUBLANES, xbc_w), F32), pltpu.VMEM((SSM_STATE, d_inner), F32)],
        compiler_params=_params("parallel", "arbitrary"),
        name="ssd",
    )(proj, proj, dt_raw, cw, conv_b.reshape(1, xbc_w),
      jnp.pad(dt_bias, (0, pad)).reshape(1, LANES), jnp.pad(a_log, (0, pad)).reshape(1, LANES),
      jnp.repeat(d_skip, SSM_HEAD_DIM).reshape(1, d_inner), norm_g.reshape(1, d_inner), expand, tri)


def _proj_res_kernel(h_ref, xa_ref, xb_ref, wa_ref, wb_ref, g_ref, o_ref):
    y = _mm(xa_ref[...], wa_ref[...]) + _mm(xb_ref[...], wb_ref[...])
    o_ref[...] = h_ref[...] + _rms(y, g_ref[...])


def _proj_res(h, xa, xb, w, g, tm=512):
    t, d = h.shape
    ka, kb = xa.shape[1], xb.shape[1]
    w = w.astype(BF16)
    row = lambda n: pl.BlockSpec((tm, n), lambda i: (i, 0))
    return pl.pallas_call(
        _proj_res_kernel,
        out_shape=jax.ShapeDtypeStruct((t, d), F32),
        grid=(t // tm,),
        in_specs=[row(d), row(ka), row(kb), _resident((ka, d)), _resident((kb, d)), _resident((1, d))],
        out_specs=row(d),
        compiler_params=_params("parallel"),
        name="proj_res",
    )(h, xa, xb, w[:ka], w[ka:], g.reshape(1, d))


MAX_DILATION = 16
SPAN = ATTN_BLOCK * MAX_DILATION
ATTN_WIDTH = ATTN_HEADS * ATTN_HEAD_DIM


def _attend(jobs, bias_ref, scores_ref):
    dh = ATTN_HEAD_DIM
    nq = ATTN_BLOCK
    npairs = ATTN_HEADS // 2
    low_head = lax.broadcasted_iota(jnp.int32, (nq, LANES), 1) < dh
    for j, (q, k, _, _) in enumerate(jobs):
        for pair in range(npairs):
            cols = slice(pair * LANES, (pair + 1) * LANES)
            qp = q[:, cols]
            q2 = jnp.concatenate([jnp.where(low_head, qp, 0), jnp.where(low_head, 0, qp)], axis=0)
            scores_ref[j * npairs + pair] = lax.dot_general(k[:, cols], q2, (((1,), (1,)), ((), ())),
                                                            preferred_element_type=F32)
    results = []
    for j, (_, _, v, table) in enumerate(jobs):
        outs, lses = [], []
        for pair in range(npairs):
            cols = slice(pair * LANES, (pair + 1) * LANES)
            st = scores_ref[j * npairs + pair] + bias_ref[table, pair]
            m = jnp.max(st, axis=0, keepdims=True)
            p = jnp.exp(st - m)
            l = jnp.sum(p, axis=0, keepdims=True)
            ot = lax.dot_general(v[:, cols], p.astype(BF16), (((0,), (0,)), ((), ())), preferred_element_type=F32)
            ot = ot / l
            lse = m + jnp.log(l)
            for hh in range(2):
                outs.append(ot[hh * dh:(hh + 1) * dh, hh * nq:(hh + 1) * nq])
                lses.append(lse[:, hh * nq:(hh + 1) * nq])
        lse_t = jnp.concatenate(lses + [jnp.zeros((LANES - ATTN_HEADS, nq), F32)], axis=0)
        results.append((jnp.concatenate(outs, axis=0).T, lse_t.T))
    return results


def _attn_pairs_kernel(q_ref, kp_ref, kc_ref, vp_ref, vc_ref, bias_ref, o_ref, lse_ref, scores_ref):
    table = jnp.minimum(pl.program_id(2), 1)
    scale = ATTN_HEAD_DIM ** -0.5
    jobs = []
    for half in range(2):
        rows = slice(half * ATTN_BLOCK, (half + 1) * ATTN_BLOCK)
        k = jnp.concatenate([kp_ref[rows, :], kc_ref[rows, :]], axis=0)
        v = jnp.concatenate([vp_ref[rows, :], vc_ref[rows, :]], axis=0)
        jobs.append((q_ref[rows, :] * scale, k, v, table))
    for half, (o, lse) in enumerate(_attend(jobs, bias_ref, scores_ref)):
        rows = slice(half * ATTN_BLOCK, (half + 1) * ATTN_BLOCK)
        o_ref[rows, :] = o
        lse_ref[rows, :] = lse


def _attn_chain_kernel(q_ref, kp_ref, kc_ref, vp_ref, vc_ref, bias_ref, o_ref, lse_ref, scores_ref, *, runs,
                       run_rows):
    half_rows = run_rows // 2
    scale = ATTN_HEAD_DIM ** -0.5

    def halves(ref):
        x = ref[...].astype(F32)
        return [x[:, i * half_rows:(i + 1) * half_rows, :].reshape(ATTN_BLOCK, x.shape[-1]).astype(BF16)
                for i in range(2)]

    q0, q1 = halves(q_ref)
    k0, k1 = halves(kc_ref)
    v0, v1 = halves(vc_ref)
    kprev = halves(kp_ref)[1]
    vprev = halves(vp_ref)[1]
    table = jnp.minimum(pl.program_id(2), 1)
    (o0, lse0), (o1, lse1) = _attend(
        [(q0 * scale, jnp.concatenate([kprev, k0], axis=0), jnp.concatenate([vprev, v0], axis=0), table),
         (q1 * scale, jnp.concatenate([k0, k1], axis=0), jnp.concatenate([v0, v1], axis=0), 1)],
        bias_ref, scores_ref)

    def interleave(a, b):
        shape = (runs, half_rows, a.shape[-1])
        return jnp.concatenate([a.reshape(shape), b.reshape(shape)], axis=1)

    o_ref[...] = interleave(o0, o1)
    lse_ref[...] = interleave(lse0, lse1)


def _attn_bias(window, dilation, offsets):
    steps = window // dilation
    slopes = jnp.exp2(-8.0 * jnp.arange(1, ATTN_HEADS + 1, dtype=F32) / ATTN_HEADS)
    kj = jnp.concatenate([offsets - ATTN_BLOCK, offsets])[:, None]
    qi = offsets[None, :]
    rel = qi - kj
    valid = (rel >= 0) & (rel <= steps)
    bias = -slopes[:, None, None] * (rel * dilation).astype(F32)
    later = jnp.where(valid[None], bias, MASK_VALUE)
    first = jnp.where((valid & (kj >= 0))[None], bias, MASK_VALUE)
    table = jnp.stack([first, later], axis=0)
    table = table.reshape(2, ATTN_HEADS // 2, 2, 2 * ATTN_BLOCK, ATTN_BLOCK)
    return table.transpose(0, 1, 3, 2, 4).reshape(2, ATTN_HEADS // 2, 2 * ATTN_BLOCK, 2 * ATTN_BLOCK)


def _attn_group(qkv, g, bsz, seq):
    window, dilation = DILATED_PATTERNS[g]
    ngroups = len(DILATED_PATTERNS)
    assert window // dilation <= ATTN_BLOCK and seq % SPAN == 0 and MAX_DILATION % dilation == 0
    t = bsz * seq
    nspan = seq // SPAN
    ncols = qkv.shape[1]
    col = lambda which: which * ngroups + g
    bias_spec = pl.BlockSpec((2, ATTN_HEADS // 2, 2 * ATTN_BLOCK, 2 * ATTN_BLOCK), lambda *_: (0, 0, 0, 0),
                             pipeline_mode=pl.Buffered(1))
    u = jnp.arange(ATTN_BLOCK)
    scores_scratch = pltpu.VMEM((ATTN_HEADS, 2 * ATTN_BLOCK, 2 * ATTN_BLOCK), F32)

    if dilation == MAX_DILATION:
        pairs = MAX_DILATION // 2
        blk = (2 * ATTN_BLOCK, ATTN_WIDTH)
        cur = lambda which: pl.BlockSpec(blk, lambda b, r, j: ((b * nspan + j) * pairs + r, col(which)))
        prev = lambda which: pl.BlockSpec(
            blk, lambda b, r, j: ((b * nspan + jnp.maximum(j - 1, 0)) * pairs + r, col(which)))
        out = lambda w: pl.BlockSpec((2 * ATTN_BLOCK, w), lambda b, r, j: ((b * nspan + j) * pairs + r, 0))
        o, lse = pl.pallas_call(
            _attn_pairs_kernel,
            out_shape=[jax.ShapeDtypeStruct((t, ATTN_WIDTH), F32), jax.ShapeDtypeStruct((t, LANES), F32)],
            grid=(bsz, pairs, nspan),
            in_specs=[cur(0), prev(1), cur(1), prev(2), cur(2), bias_spec],
            out_specs=[out(ATTN_WIDTH), out(LANES)],
            scratch_shapes=[scores_scratch],
            compiler_params=_params("parallel", "parallel", "arbitrary"),
            name=f"attn_d{dilation}",
        )(qkv, qkv, qkv, qkv, qkv, _attn_bias(window, dilation, u))
        return o, lse

    runs = MAX_DILATION // dilation
    run_rows = 2 * ATTN_BLOCK // runs
    steps_per_span = ATTN_BLOCK // run_rows
    shape = (bsz * nspan, runs, dilation, steps_per_span, run_rows)
    blk = lambda w: (None, runs, None, None, run_rows, w)
    at = lambda b, r, s, c: (b * nspan + s // steps_per_span, 0, r, s % steps_per_span, 0, c)
    cur = lambda which: pl.BlockSpec(blk(ATTN_WIDTH), lambda b, r, s: at(b, r, s, col(which)))
    prev = lambda which: pl.BlockSpec(blk(ATTN_WIDTH), lambda b, r, s: at(b, r, jnp.maximum(s - 1, 0), col(which)))
    out = lambda w: pl.BlockSpec(blk(w), lambda b, r, s: at(b, r, s, 0))
    view = qkv.reshape(shape + (ncols,))
    half_rows = run_rows // 2
    offsets = runs * (u % half_rows) + u // half_rows
    o, lse = pl.pallas_call(
        functools.partial(_attn_chain_kernel, runs=runs, run_rows=run_rows),
        out_shape=[jax.ShapeDtypeStruct(shape + (ATTN_WIDTH,), F32), jax.ShapeDtypeStruct(shape + (LANES,), F32)],
        grid=(bsz, dilation, nspan * steps_per_span),
        in_specs=[cur(0), prev(1), cur(1), prev(2), cur(2), bias_spec],
        out_specs=[out(ATTN_WIDTH), out(LANES)],
        scratch_shapes=[scores_scratch],
        compiler_params=_params("parallel", "parallel", "arbitrary"),
        name=f"attn_d{dilation}",
    )(view, view, view, view, view, _attn_bias(window, dilation, offsets))
    return o.reshape(t, ATTN_WIDTH), lse.reshape(t, LANES)


def _merge_proj_kernel(h_ref, o0_ref, o1_ref, o2_ref, l0_ref, l1_ref, l2_ref, expand_ref, w_ref, g_ref, out_ref):
    l0, l1, l2 = l0_ref[...], l1_ref[...], l2_ref[...]
    m = jnp.maximum(jnp.maximum(l0, l1), l2)
    e0, e1, e2 = jnp.exp(l0 - m), jnp.exp(l1 - m), jnp.exp(l2 - m)
    inv = 1.0 / (e0 + e1 + e2)
    merged = None
    for e, o_ref in ((e0, o0_ref), (e1, o1_ref), (e2, o2_ref)):
        wgt = e * inv
        w_hi = wgt.astype(BF16)
        w_lo = (wgt - w_hi.astype(F32)).astype(BF16)
        term = (_mm(w_hi, expand_ref[...]) + _mm(w_lo, expand_ref[...])) * o_ref[...]
        merged = term if merged is None else merged + term
    y = _mm(merged.astype(BF16), w_ref[...])
    out_ref[...] = h_ref[...] + _rms(y, g_ref[...])


def _merge_proj(h, outs, lses, w, g, tm=512):
    t, d = h.shape
    k = w.shape[0]
    row = lambda n: pl.BlockSpec((tm, n), lambda i: (i, 0))
    head_of_col = jnp.arange(k) // ATTN_HEAD_DIM
    expand = (jnp.arange(LANES)[:, None] == head_of_col[None, :]).astype(BF16)
    return pl.pallas_call(
        _merge_proj_kernel,
        out_shape=jax.ShapeDtypeStruct((t, d), F32),
        grid=(t // tm,),
        in_specs=[row(d)] + [row(k)] * 3 + [row(LANES)] * 3
        + [_resident((LANES, k)), _resident((k, d)), _resident((1, d))],
        out_specs=row(d),
        compiler_params=_params("parallel"),
        name="merge_proj",
    )(h, *outs, *lses, expand, w.astype(BF16), g.reshape(1, d))


def _conv_ssd_mixer(h, bsz, seq, g_pre, g_post, w_in, conv_dw_w, conv_dw_b, conv_ln_g, conv_ln_b,
                    ssm_conv_w, ssm_conv_b, ssm_dt_bias, ssm_a_log, ssm_d, ssm_norm_g, w_out):
    conv_ch = conv_dw_w.shape[1]
    d_inner = SSM_HEADS * SSM_HEAD_DIM
    xbc_w = ssm_conv_w.shape[1]
    main = 2 * conv_ch + d_inner + xbc_w
    w = jnp.pad(w_in, ((0, 0), (0, LANES - SSM_HEADS))).astype(BF16)
    proj, dt_raw = _rms_matmul(h, g_pre, w, (main, LANES), F32)
    proj = proj.reshape(bsz, seq, main)
    dt_raw = dt_raw.reshape(bsz, seq, LANES)
    assert (2 * conv_ch) % d_inner == 0 and (2 * conv_ch + d_inner) % xbc_w == 0
    ya = _convmod(proj, conv_dw_w, conv_dw_b, conv_ln_g, conv_ln_b)
    yb = _ssd(proj, dt_raw, ssm_conv_w, ssm_conv_b, ssm_dt_bias, ssm_a_log, ssm_d, ssm_norm_g,
              z_col=(2 * conv_ch) // d_inner, xbc_col=(2 * conv_ch + d_inner) // xbc_w)
    t = bsz * seq
    return _proj_res(h, ya.reshape(t, conv_ch), yb.reshape(t, d_inner), w_out, g_post)


def _attention_mixer(h, bsz, seq, g_pre, g_post, w_qkv, w_o):
    n = w_qkv.shape[1]
    t, d = h.shape
    h = h.reshape(t // SPAN, ATTN_BLOCK, MAX_DILATION, d).transpose(0, 2, 1, 3).reshape(t, d)
    (qkv,) = _rms_matmul(h, g_pre, w_qkv.astype(BF16), (n,), BF16)
    outs, lses = [], []
    for g in range(len(DILATED_PATTERNS)):
        o, lse = _attn_group(qkv, g, bsz, seq)
        outs.append(o)
        lses.append(lse)
    h = _merge_proj(h, outs, lses, w_o, g_post)
    return h.reshape(t // SPAN, MAX_DILATION, ATTN_BLOCK, d).transpose(0, 2, 1, 3).reshape(t, d)


def kernel(x, norm_g, ffn_w1, ffn_w2, hyb_w_in, conv_dw_w, conv_dw_b, conv_ln_g, conv_ln_b, ssm_conv_w, ssm_conv_b, ssm_dt_bias, ssm_a_log, ssm_d, ssm_norm_g, hyb_w_out, attn_w_qkv, attn_w_o):
    bsz, seq, d = x.shape
    h = x.reshape(bsz * seq, d)
    for i in range(norm_g.shape[0]):
        g = norm_g[i]
        j = i // 2
        h = _ffn(h, g[0], g[1], ffn_w1[i, 0], ffn_w2[i, 0])
        if i % 2 == 0:
            h = _conv_ssd_mixer(h, bsz, seq, g[2], g[3], hyb_w_in[j], conv_dw_w[j], conv_dw_b[j],
                                conv_ln_g[j], conv_ln_b[j], ssm_conv_w[j], ssm_conv_b[j], ssm_dt_bias[j],
                                ssm_a_log[j], ssm_d[j], ssm_norm_g[j], hyb_w_out[j])
        else:
            h = _attention_mixer(h, bsz, seq, g[2], g[3], attn_w_qkv[j], attn_w_o[j])
        h = _ffn(h, g[4], g[5], ffn_w1[i, 1], ffn_w2[i, 1])
    return h.reshape(bsz, seq, d)
```

```python
import functools

import jax
import jax.numpy as jnp
from jax import lax
from jax.experimental import pallas as pl
from jax.experimental.pallas import tpu as pltpu

NORM_EPS = 1e-6
CONV_WIDTH = 31
SSM_HEAD_DIM = 64
SSM_HEADS = 16
SSM_GROUPS = 2
SSM_STATE = 128
SSM_CONV_WIDTH = 4
SSM_CHUNK = 128
ATTN_HEAD_DIM = 64
ATTN_HEADS = 8
DILATED_PATTERNS = ((128, 1), (512, 4), (2048, 16))
ATTN_BLOCK = 128

LANES = 128
SUBLANES = 8
VMEM_LIMIT_BYTES = 56 * 1024 * 1024
MASK_VALUE = -1e30

F32 = jnp.float32
BF16 = jnp.bfloat16


def _params(*sem):
    return pltpu.CompilerParams(dimension_semantics=sem, vmem_limit_bytes=VMEM_LIMIT_BYTES)


def _resident(shape):
    nd = len(shape)
    return pl.BlockSpec(shape, lambda *_: (0,) * nd, pipeline_mode=pl.Buffered(1))


def _rms(x, g):
    return x * lax.rsqrt(jnp.mean(x * x, axis=-1, keepdims=True) + NORM_EPS) * g


def _silu(x):
    return x * jax.nn.sigmoid(x)


def _mm(a, b):
    return jnp.dot(a, b, preferred_element_type=F32)


def _ffn_kernel(h_ref, gpre_ref, gpost_ref, wg_ref, wu_ref, w2_ref, o_ref):
    h = h_ref[...]
    u = _rms(h, gpre_ref[...]).astype(BF16)
    gate = _mm(u, wg_ref[...])
    up = _mm(u, wu_ref[...])
    act = (_silu(gate) * up).astype(BF16)
    y = _mm(act, w2_ref[...])
    o_ref[...] = h + 0.5 * _rms(y, gpost_ref[...])


def _ffn(h, g_pre, g_post, w1, w2, tm=512):
    t, d = h.shape
    f = w2.shape[0]
    wg = w1[:, :f].astype(BF16)
    wu = w1[:, f:].astype(BF16)
    row = pl.BlockSpec((tm, d), lambda i: (i, 0))
    return pl.pallas_call(
        _ffn_kernel,
        out_shape=jax.ShapeDtypeStruct((t, d), F32),
        grid=(t // tm,),
        in_specs=[row, _resident((1, d)), _resident((1, d)),
                  _resident((d, f)), _resident((d, f)), _resident((f, d))],
        out_specs=row,
        compiler_params=_params("parallel"),
        name="ffn",
    )(h, g_pre.reshape(1, d), g_post.reshape(1, d), wg, wu, w2.astype(BF16))


RESIDUES_PER_STEP = SUBLANES


def _gather_residues(h_ref):
    return jnp.concatenate([h_ref[:, r, :] for r in range(h_ref.shape[1])], axis=0)


def _qkv_kernel(h_ref, g_ref, w_ref, o_ref, u_ref):
    @pl.when(pl.program_id(1) == 0)
    def _():
        u_ref[...] = _rms(_gather_residues(h_ref), g_ref[...]).astype(BF16)

    n = o_ref.shape[1]
    col = pl.multiple_of(pl.program_id(1) * n, LANES)
    o_ref[...] = _mm(u_ref[...], w_ref[:, pl.ds(col, n)]).astype(o_ref.dtype)


def _qkv_proj(h, g, w, nsplit=3):
    t, d = h.shape
    n = w.shape[1]
    rows = RESIDUES_PER_STEP * ATTN_BLOCK
    per_span = MAX_DILATION // RESIDUES_PER_STEP
    view = h.reshape(t // SPAN, ATTN_BLOCK, MAX_DILATION, d)
    return pl.pallas_call(
        _qkv_kernel,
        out_shape=jax.ShapeDtypeStruct((t, n), BF16),
        grid=(t // rows, nsplit),
        in_specs=[pl.BlockSpec((None, ATTN_BLOCK, RESIDUES_PER_STEP, d),
                               lambda i, j: (i // per_span, 0, i % per_span, 0)),
                  _resident((1, d)), _resident((d, n))],
        out_specs=pl.BlockSpec((rows, n // nsplit), lambda i, j: (i, j)),
        scratch_shapes=[pltpu.VMEM((rows, d), BF16)],
        compiler_params=_params("parallel", "arbitrary"),
        name="qkv_proj",
    )(view, g.reshape(1, d), w)


def _hyb_in_kernel(h_ref, g_ref, wx_ref, wg_ref, wz_ref, wd_ref, cw_ref, cb_ref, dtb_ref,
                   xbc_ref, glu_ref, zs_ref, dt_ref, xbuf_ref, *, tiles_per_seq):
    tm = h_ref.shape[0]
    u = _rms(h_ref[...], g_ref[...]).astype(BF16)
    i = pl.program_id(0)

    @pl.when(i % tiles_per_seq == 0)
    def _():
        xbuf_ref[0:SUBLANES, :] = jnp.zeros((SUBLANES, xbuf_ref.shape[1]), F32)

    @pl.when(i % tiles_per_seq != 0)
    def _():
        xbuf_ref[0:SUBLANES, :] = xbuf_ref[tm:tm + SUBLANES, :]

    xbuf_ref[SUBLANES:SUBLANES + tm, :] = _mm(u, wx_ref[...])
    first = SUBLANES - (SSM_CONV_WIDTH - 1)
    acc = jnp.broadcast_to(cb_ref[...], (tm, xbuf_ref.shape[1]))
    for k in range(SSM_CONV_WIDTH):
        acc = acc + xbuf_ref[first + k:first + k + tm, :] * cw_ref[k:k + 1, :]
    xbc_ref[...] = _silu(acc)

    ch = glu_ref.shape[1]
    vg = _mm(u, wg_ref[...])
    glu_ref[...] = vg[:, :ch] * jax.nn.sigmoid(vg[:, ch:])
    zs_ref[...] = _silu(_mm(u, wz_ref[...]))
    dt_ref[...] = jax.nn.softplus(_mm(u, wd_ref[...]) + dtb_ref[...])


def _hyb_in(h, g, w_in, conv_ch, d_inner, ssm_conv_w, ssm_conv_b, dt_bias, seq, tm=512):
    t, d = h.shape
    xbc_w = ssm_conv_w.shape[1]
    s0, s1, s2 = 2 * conv_ch, 2 * conv_ch + d_inner, 2 * conv_ch + d_inner + xbc_w
    pad = LANES - SSM_HEADS
    w = w_in.astype(BF16)
    cw = jnp.zeros((SUBLANES, xbc_w), F32).at[:SSM_CONV_WIDTH].set(ssm_conv_w)
    row = lambda n: pl.BlockSpec((tm, n), lambda i: (i, 0))
    widths = (xbc_w, conv_ch, d_inner, LANES)
    return pl.pallas_call(
        functools.partial(_hyb_in_kernel, tiles_per_seq=seq // tm),
        out_shape=[jax.ShapeDtypeStruct((t, n), F32) for n in widths],
        grid=(t // tm,),
        in_specs=[row(d), _resident((1, d)), _resident((d, xbc_w)), _resident((d, s0)),
                  _resident((d, d_inner)), _resident((d, LANES)),
                  _resident((SUBLANES, xbc_w)), _resident((1, xbc_w)), _resident((1, LANES))],
        out_specs=[row(n) for n in widths],
        scratch_shapes=[pltpu.VMEM((tm + SUBLANES, xbc_w), F32)],
        compiler_params=_params("arbitrary"),
        name="hyb_in",
    )(h, g.reshape(1, d), w[:, s1:s2], w[:, :s0], w[:, s0:s1], jnp.pad(w[:, s2:], ((0, 0), (0, pad))),
      cw, ssm_conv_b.reshape(1, xbc_w), jnp.pad(dt_bias, (0, pad)).reshape(1, LANES))


CONV_TAIL = 32
CONV_ROW_SPLIT = 2


def _convmod_kernel(glu_ref, w_ref, b_ref, lg_ref, lb_ref, o_ref, buf_ref, hid_ref, *, tl):
    ch = glu_ref.shape[-1]
    nt = ch // LANES

    @pl.when(pl.program_id(1) == 0)
    def _():
        buf_ref[0, :, 0:CONV_TAIL, :] = jnp.zeros((nt, CONV_TAIL, LANES), F32)

    @pl.when(pl.program_id(1) > 0)
    def _():
        buf_ref[0, :, 0:CONV_TAIL, :] = buf_ref[0, :, tl:tl + CONV_TAIL, :]

    for c in range(nt):
        buf_ref[0, c, CONV_TAIL:CONV_TAIL + tl, :] = glu_ref[:, c * LANES:(c + 1) * LANES]

    first = CONV_TAIL - (CONV_WIDTH - 1)
    shifted_rows = tl + CONV_TAIL - SUBLANES
    rows = tl // CONV_ROW_SPLIT
    tiles = (rows // SUBLANES, SUBLANES, LANES)

    def lane_tile(c, carry):
        for r in range(1, SUBLANES):
            buf_ref[r, c, 0:shifted_rows, :] = buf_ref[0, c, r:r + shifted_rows, :]
        wc = w_ref[c]
        wk = [jnp.broadcast_to(wc[k:k + 1, :], (SUBLANES, LANES))[None] for k in range(CONV_WIDTH)]
        for part in range(CONV_ROW_SPLIT):
            acc = jnp.broadcast_to(b_ref[c], tiles)
            for k in range(CONV_WIDTH):
                shift = (first + k) % SUBLANES
                row = part * rows + first + k - shift
                acc = acc + buf_ref[shift, c, row:row + rows, :].reshape(tiles) * wk[k]
            hid_ref[c, part * rows:(part + 1) * rows, :] = acc.reshape(rows, LANES)
        return carry

    lax.fori_loop(0, nt, lane_tile, 0)

    hid = jnp.concatenate([hid_ref[c] for c in range(nt)], axis=1)
    xc = hid - jnp.mean(hid, axis=-1, keepdims=True)
    y = xc * lax.rsqrt(jnp.mean(xc * xc, axis=-1, keepdims=True) + NORM_EPS)
    o_ref[...] = _silu(y * lg_ref[...] + lb_ref[...]).astype(o_ref.dtype)


def _convmod(glu, w, b, ln_g, ln_b, tl=256):
    bsz, seq, ch = glu.shape
    nt = ch // LANES
    wpad = jnp.zeros((CONV_TAIL, ch), F32).at[:CONV_WIDTH].set(w)
    wpad = wpad.reshape(CONV_TAIL, nt, LANES).transpose(1, 0, 2)
    vec = _resident((1, ch))
    return pl.pallas_call(
        functools.partial(_convmod_kernel, tl=tl),
        out_shape=jax.ShapeDtypeStruct((bsz, seq, ch), BF16),
        grid=(bsz, seq // tl),
        in_specs=[pl.BlockSpec((None, tl, ch), lambda bi, i: (bi, i, 0)),
                  _resident((nt, CONV_TAIL, LANES)), _resident((nt, 1, LANES)), vec, vec],
        out_specs=pl.BlockSpec((None, tl, ch), lambda bi, i: (bi, i, 0)),
        scratch_shapes=[pltpu.VMEM((SUBLANES, nt, tl + CONV_TAIL, LANES), F32), pltpu.VMEM((nt, tl, LANES), F32)],
        compiler_params=_params("parallel", "arbitrary"),
        name="convmod",
    )(glu, wpad, b.reshape(nt, 1, LANES), ln_g.reshape(1, ch), ln_b.reshape(1, ch))


SSD_CHUNKS_PER_STEP = 4


def _ssd_kernel(xbc_ref, zs_ref, dt_ref, alog_ref, dskip_ref, ng_ref, expand_ref, tri_ref, o_ref, state_ref):
    q = SSM_CHUNK
    nchunk = xbc_ref.shape[0] // q
    d_inner = SSM_HEADS * SSM_HEAD_DIM
    gcols = d_inner // SSM_GROUPS
    nst = SSM_STATE
    chunk = lambda c: slice(c * q, (c + 1) * q)

    @pl.when(pl.program_id(1) == 0)
    def _():
        state_ref[...] = jnp.zeros(state_ref.shape, F32)

    dt = dt_ref[...]
    a = dt * (-jnp.exp(alog_ref[...]))
    tri = tri_ref[...]
    a_hi = a.astype(BF16)
    a_mid = (a - a_hi.astype(F32)).astype(BF16)
    a_lo = (a - a_hi.astype(F32) - a_mid.astype(F32)).astype(BF16)
    acums = [_mm(tri, a_hi[chunk(c)]) + _mm(tri, a_mid[chunk(c)]) + _mm(tri, a_lo[chunk(c)])
             for c in range(nchunk)]
    acum = jnp.concatenate(acums, axis=0)
    dstate = jnp.concatenate([jnp.exp(ac[q - 1:q, :] - ac) for ac in acums], axis=0)
    per_head = jnp.concatenate([dt, jnp.exp(acum), dstate], axis=0)
    ph_hi = per_head.astype(BF16)
    ph_lo = (per_head - ph_hi.astype(F32)).astype(BF16)
    full = _mm(ph_hi, expand_ref[...]) + _mm(ph_lo, expand_ref[...])
    rows = nchunk * q
    dt_full = full[0:rows]
    eacum_full = full[rows:2 * rows]
    dstate_full = full[2 * rows:3 * rows]

    xs = xbc_ref[:, :d_inner]
    xdt = xs * dt_full
    xw = (xdt * dstate_full).astype(BF16)
    causal = lax.broadcasted_iota(jnp.int32, (q, q), 0) >= lax.broadcasted_iota(jnp.int32, (q, q), 1)
    low_half = lax.broadcasted_iota(jnp.int32, (q, LANES), 1) < SSM_HEAD_DIM
    heads_per_group = SSM_HEADS // SSM_GROUPS

    state = state_ref[...]
    for c in range(nchunk):
        r = chunk(c)
        ac = acums[c]
        ac_t = ac.T
        y_diag, y_off, new_states = [], [], []
        for g in range(SSM_GROUPS):
            bg = xbc_ref[r, d_inner + g * nst:d_inner + (g + 1) * nst].astype(BF16)
            cg = xbc_ref[r, d_inner + (SSM_GROUPS + g) * nst:d_inner + (SSM_GROUPS + g + 1) * nst].astype(BF16)
            gsl = slice(g * gcols, (g + 1) * gcols)
            cb = lax.dot_general(cg, bg, (((1,), (1,)), ((), ())), preferred_element_type=F32)
            y_off.append(_mm(cg, state[:, gsl].astype(BF16)))
            new_states.append(lax.dot_general(bg, xw[r, gsl], (((0,), (0,)), ((), ())),
                                              preferred_element_type=F32))
            for pair in range(heads_per_group // 2):
                h0 = g * heads_per_group + 2 * pair
                masks = []
                for h in (h0, h0 + 1):
                    diff = ac[:, h:h + 1] - ac_t[h:h + 1, :]
                    masks.append((cb * jnp.exp(jnp.where(causal, diff, MASK_VALUE))).astype(BF16))
                lhs = jnp.concatenate(masks, axis=1)
                xp = xdt[r, h0 * SSM_HEAD_DIM:(h0 + 2) * SSM_HEAD_DIM]
                rhs = jnp.concatenate([jnp.where(low_half, xp, 0.0), jnp.where(low_half, 0.0, xp)],
                                      axis=0).astype(BF16)
                y_diag.append(_mm(lhs, rhs))
        decay = eacum_full[r]
        y = (jnp.concatenate(y_diag, axis=1) + jnp.concatenate(y_off, axis=1) * decay
             + dskip_ref[...] * xs[r])
        state = state * decay[q - 1:q, :] + jnp.concatenate(new_states, axis=1)
        y = y * zs_ref[r, :]
        o_ref[r, :] = _rms(y, ng_ref[...]).astype(o_ref.dtype)
    state_ref[...] = state


def _ssd(xbc, zs, dt, a_log, d_skip, norm_g):
    bsz, seq, xbc_w = xbc.shape
    q = SSM_CHUNK
    rows = SSD_CHUNKS_PER_STEP * q
    d_inner = SSM_HEADS * SSM_HEAD_DIM
    head_of_col = jnp.arange(d_inner) // SSM_HEAD_DIM
    expand = (jnp.arange(LANES)[:, None] == head_of_col[None, :]).astype(BF16)
    tri = (jnp.arange(q)[:, None] >= jnp.arange(q)[None, :]).astype(BF16)
    return pl.pallas_call(
        _ssd_kernel,
        out_shape=jax.ShapeDtypeStruct((bsz, seq, d_inner), BF16),
        grid=(bsz, seq // rows),
        in_specs=[pl.BlockSpec((None, rows, xbc_w), lambda bi, i: (bi, i, 0)),
                  pl.BlockSpec((None, rows, d_inner), lambda bi, i: (bi, i, 0)),
                  pl.BlockSpec((None, rows, LANES), lambda bi, i: (bi, i, 0)),
                  _resident((1, LANES)), _resident((1, d_inner)), _resident((1, d_inner)),
                  _resident((LANES, d_inner)), _resident((q, q))],
        out_specs=pl.BlockSpec((None, rows, d_inner), lambda bi, i: (bi, i, 0)),
        scratch_shapes=[pltpu.VMEM((SSM_STATE, d_inner), F32)],
        compiler_params=_params("parallel", "arbitrary"),
        name="ssd",
    )(xbc, zs, dt, jnp.pad(a_log, (0, LANES - SSM_HEADS)).reshape(1, LANES),
      jnp.repeat(d_skip, SSM_HEAD_DIM).reshape(1, d_inner), norm_g.reshape(1, d_inner), expand, tri)


def _proj_res_kernel(h_ref, xa_ref, xb_ref, wa_ref, wb_ref, g_ref, o_ref):
    y = _mm(xa_ref[...], wa_ref[...]) + _mm(xb_ref[...], wb_ref[...])
    o_ref[...] = h_ref[...] + _rms(y, g_ref[...])


def _proj_res(h, xa, xb, w, g, tm=512):
    t, d = h.shape
    ka, kb = xa.shape[1], xb.shape[1]
    w = w.astype(BF16)
    row = lambda n: pl.BlockSpec((tm, n), lambda i: (i, 0))
    return pl.pallas_call(
        _proj_res_kernel,
        out_shape=jax.ShapeDtypeStruct((t, d), F32),
        grid=(t // tm,),
        in_specs=[row(d), row(ka), row(kb), _resident((ka, d)), _resident((kb, d)), _resident((1, d))],
        out_specs=row(d),
        compiler_params=_params("parallel"),
        name="proj_res",
    )(h, xa, xb, w[:ka], w[ka:], g.reshape(1, d))


MAX_DILATION = 16
SPAN = ATTN_BLOCK * MAX_DILATION
ATTN_WIDTH = ATTN_HEADS * ATTN_HEAD_DIM


def _attend(jobs, bias_ref, scores_ref):
    dh = ATTN_HEAD_DIM
    nq = ATTN_BLOCK
    npairs = ATTN_HEADS // 2
    low_head = lax.broadcasted_iota(jnp.int32, (nq, LANES), 1) < dh
    for j, (q, k, _, _) in enumerate(jobs):
        for pair in range(npairs):
            cols = slice(pair * LANES, (pair + 1) * LANES)
            qp = q[:, cols]
            q2 = jnp.concatenate([jnp.where(low_head, qp, 0), jnp.where(low_head, 0, qp)], axis=0)
            scores_ref[j * npairs + pair] = lax.dot_general(k[:, cols], q2, (((1,), (1,)), ((), ())),
                                                            preferred_element_type=F32)
    results = []
    for j, (_, _, v, table) in enumerate(jobs):
        outs, lses = [], []
        for pair in range(npairs):
            cols = slice(pair * LANES, (pair + 1) * LANES)
            st = scores_ref[j * npairs + pair] + bias_ref[table, pair]
            m = jnp.max(st, axis=0, keepdims=True)
            p = jnp.exp(st - m)
            l = jnp.sum(p, axis=0, keepdims=True)
            ot = lax.dot_general(v[:, cols], p.astype(BF16), (((0,), (0,)), ((), ())), preferred_element_type=F32)
            ot = ot / l
            lse = m + jnp.log(l)
            for hh in range(2):
                outs.append(ot[hh * dh:(hh + 1) * dh, hh * nq:(hh + 1) * nq])
                lses.append(lse[:, hh * nq:(hh + 1) * nq])
        lse_t = jnp.concatenate(lses + [jnp.zeros((LANES - ATTN_HEADS, nq), F32)], axis=0)
        results.append((jnp.concatenate(outs, axis=0).T, lse_t.T))
    return results


def _attn_pairs_kernel(q_ref, kp_ref, kc_ref, vp_ref, vc_ref, bias_ref, o_ref, lse_ref, scores_ref):
    table = jnp.minimum(pl.program_id(2), 1)
    scale = ATTN_HEAD_DIM ** -0.5
    jobs = []
    for half in range(2):
        rows = slice(half * ATTN_BLOCK, (half + 1) * ATTN_BLOCK)
        k = jnp.concatenate([kp_ref[rows, :], kc_ref[rows, :]], axis=0)
        v = jnp.concatenate([vp_ref[rows, :], vc_ref[rows, :]], axis=0)
        jobs.append((q_ref[rows, :] * scale, k, v, table))
    for half, (o, lse) in enumerate(_attend(jobs, bias_ref, scores_ref)):
        rows = slice(half * ATTN_BLOCK, (half + 1) * ATTN_BLOCK)
        o_ref[rows, :] = o
        lse_ref[rows, :] = lse


def _attn_chain_kernel(q_ref, kp_ref, kc_ref, vp_ref, vc_ref, bias_ref, o_ref, lse_ref, scores_ref, *, runs,
                       run_rows):
    half_rows = run_rows // 2
    scale = ATTN_HEAD_DIM ** -0.5

    def halves(ref):
        x = ref[...].astype(F32)
        return [x[:, i * half_rows:(i + 1) * half_rows, :].reshape(ATTN_BLOCK, x.shape[-1]).astype(BF16)
                for i in range(2)]

    q0, q1 = halves(q_ref)
    k0, k1 = halves(kc_ref)
    v0, v1 = halves(vc_ref)
    kprev = halves(kp_ref)[1]
    vprev = halves(vp_ref)[1]
    table = jnp.minimum(pl.program_id(2), 1)
    (o0, lse0), (o1, lse1) = _attend(
        [(q0 * scale, jnp.concatenate([kprev, k0], axis=0), jnp.concatenate([vprev, v0], axis=0), table),
         (q1 * scale, jnp.concatenate([k0, k1], axis=0), jnp.concatenate([v0, v1], axis=0), 1)],
        bias_ref, scores_ref)

    def interleave(a, b):
        shape = (runs, half_rows, a.shape[-1])
        return jnp.concatenate([a.reshape(shape), b.reshape(shape)], axis=1)

    o_ref[...] = interleave(o0, o1)
    lse_ref[...] = interleave(lse0, lse1)


def _attn_bias(window, dilation, offsets):
    steps = window // dilation
    slopes = jnp.exp2(-8.0 * jnp.arange(1, ATTN_HEADS + 1, dtype=F32) / ATTN_HEADS)
    kj = jnp.concatenate([offsets - ATTN_BLOCK, offsets])[:, None]
    qi = offsets[None, :]
    rel = qi - kj
    valid = (rel >= 0) & (rel <= steps)
    bias = -slopes[:, None, None] * (rel * dilation).astype(F32)
    later = jnp.where(valid[None], bias, MASK_VALUE)
    first = jnp.where((valid & (kj >= 0))[None], bias, MASK_VALUE)
    table = jnp.stack([first, later], axis=0)
    table = table.reshape(2, ATTN_HEADS // 2, 2, 2 * ATTN_BLOCK, ATTN_BLOCK)
    return table.transpose(0, 1, 3, 2, 4).reshape(2, ATTN_HEADS // 2, 2 * ATTN_BLOCK, 2 * ATTN_BLOCK)


def _attn_group(qkv, g, bsz, seq):
    window, dilation = DILATED_PATTERNS[g]
    ngroups = len(DILATED_PATTERNS)
    assert window // dilation <= ATTN_BLOCK and seq % SPAN == 0 and MAX_DILATION % dilation == 0
    t = bsz * seq
    nspan = seq // SPAN
    ncols = qkv.shape[1]
    col = lambda which: which * ngroups + g
    bias_spec = pl.BlockSpec((2, ATTN_HEADS // 2, 2 * ATTN_BLOCK, 2 * ATTN_BLOCK), lambda *_: (0, 0, 0, 0),
                             pipeline_mode=pl.Buffered(1))
    u = jnp.arange(ATTN_BLOCK)
    scores_scratch = pltpu.VMEM((ATTN_HEADS, 2 * ATTN_BLOCK, 2 * ATTN_BLOCK), F32)

    if dilation == MAX_DILATION:
        pairs = MAX_DILATION // 2
        blk = (2 * ATTN_BLOCK, ATTN_WIDTH)
        cur = lambda which: pl.BlockSpec(blk, lambda b, r, j: ((b * nspan + j) * pairs + r, col(which)))
        prev = lambda which: pl.BlockSpec(
            blk, lambda b, r, j: ((b * nspan + jnp.maximum(j - 1, 0)) * pairs + r, col(which)))
        out = lambda w: pl.BlockSpec((2 * ATTN_BLOCK, w), lambda b, r, j: ((b * nspan + j) * pairs + r, 0))
        o, lse = pl.pallas_call(
            _attn_pairs_kernel,
            out_shape=[jax.ShapeDtypeStruct((t, ATTN_WIDTH), F32), jax.ShapeDtypeStruct((t, LANES), F32)],
            grid=(bsz, pairs, nspan),
            in_specs=[cur(0), prev(1), cur(1), prev(2), cur(2), bias_spec],
            out_specs=[out(ATTN_WIDTH), out(LANES)],
            scratch_shapes=[scores_scratch],
            compiler_params=_params("parallel", "parallel", "arbitrary"),
            name=f"attn_d{dilation}",
        )(qkv, qkv, qkv, qkv, qkv, _attn_bias(window, dilation, u))
        return o, lse

    runs = MAX_DILATION // dilation
    run_rows = 2 * ATTN_BLOCK // runs
    steps_per_span = ATTN_BLOCK // run_rows
    shape = (bsz * nspan, runs, dilation, steps_per_span, run_rows)
    blk = lambda w: (None, runs, None, None, run_rows, w)
    at = lambda b, r, s, c: (b * nspan + s // steps_per_span, 0, r, s % steps_per_span, 0, c)
    cur = lambda which: pl.BlockSpec(blk(ATTN_WIDTH), lambda b, r, s: at(b, r, s, col(which)))
    prev = lambda which: pl.BlockSpec(blk(ATTN_WIDTH), lambda b, r, s: at(b, r, jnp.maximum(s - 1, 0), col(which)))
    out = lambda w: pl.BlockSpec(blk(w), lambda b, r, s: at(b, r, s, 0))
    view = qkv.reshape(shape + (ncols,))
    half_rows = run_rows // 2
    offsets = runs * (u % half_rows) + u // half_rows
    o, lse = pl.pallas_call(
        functools.partial(_attn_chain_kernel, runs=runs, run_rows=run_rows),
        out_shape=[jax.ShapeDtypeStruct(shape + (ATTN_WIDTH,), F32), jax.ShapeDtypeStruct(shape + (LANES,), F32)],
        grid=(bsz, dilation, nspan * steps_per_span),
        in_specs=[cur(0), prev(1), cur(1), prev(2), cur(2), bias_spec],
        out_specs=[out(ATTN_WIDTH), out(LANES)],
        scratch_shapes=[scores_scratch],
        compiler_params=_params("parallel", "parallel", "arbitrary"),
        name=f"attn_d{dilation}",
    )(view, view, view, view, view, _attn_bias(window, dilation, offsets))
    return o.reshape(t, ATTN_WIDTH), lse.reshape(t, LANES)


def _merge_proj_kernel(h_ref, o0_ref, o1_ref, o2_ref, l0_ref, l1_ref, l2_ref, expand_ref, w_ref, g_ref, out_ref):
    l0, l1, l2 = l0_ref[...], l1_ref[...], l2_ref[...]
    m = jnp.maximum(jnp.maximum(l0, l1), l2)
    e0, e1, e2 = jnp.exp(l0 - m), jnp.exp(l1 - m), jnp.exp(l2 - m)
    inv = 1.0 / (e0 + e1 + e2)
    merged = None
    for e, o_ref in ((e0, o0_ref), (e1, o1_ref), (e2, o2_ref)):
        wgt = e * inv
        w_hi = wgt.astype(BF16)
        w_lo = (wgt - w_hi.astype(F32)).astype(BF16)
        term = (_mm(w_hi, expand_ref[...]) + _mm(w_lo, expand_ref[...])) * o_ref[...]
        merged = term if merged is None else merged + term
    y = _mm(merged.astype(BF16), w_ref[...])
    new_h = _gather_residues(h_ref) + _rms(y, g_ref[...])
    for r in range(out_ref.shape[1]):
        out_ref[:, r, :] = new_h[r * ATTN_BLOCK:(r + 1) * ATTN_BLOCK, :]


def _merge_proj(h, outs, lses, w, g):
    t, d = h.shape
    k = w.shape[0]
    rows = RESIDUES_PER_STEP * ATTN_BLOCK
    per_span = MAX_DILATION // RESIDUES_PER_STEP
    row = lambda n: pl.BlockSpec((rows, n), lambda i: (i, 0))
    tokens = pl.BlockSpec((None, ATTN_BLOCK, RESIDUES_PER_STEP, d), lambda i: (i // per_span, 0, i % per_span, 0))
    head_of_col = jnp.arange(k) // ATTN_HEAD_DIM
    expand = (jnp.arange(LANES)[:, None] == head_of_col[None, :]).astype(BF16)
    view = (t // SPAN, ATTN_BLOCK, MAX_DILATION, d)
    return pl.pallas_call(
        _merge_proj_kernel,
        out_shape=jax.ShapeDtypeStruct(view, F32),
        grid=(t // rows,),
        in_specs=[tokens] + [row(k)] * 3 + [row(LANES)] * 3
        + [_resident((LANES, k)), _resident((k, d)), _resident((1, d))],
        out_specs=tokens,
        compiler_params=_params("parallel"),
        name="merge_proj",
    )(h.reshape(view), *outs, *lses, expand, w.astype(BF16), g.reshape(1, d)).reshape(t, d)


def _conv_ssd_mixer(h, bsz, seq, g_pre, g_post, w_in, conv_dw_w, conv_dw_b, conv_ln_g, conv_ln_b,
                    ssm_conv_w, ssm_conv_b, ssm_dt_bias, ssm_a_log, ssm_d, ssm_norm_g, w_out):
    conv_ch = conv_dw_w.shape[1]
    d_inner = SSM_HEADS * SSM_HEAD_DIM
    xbc_w = ssm_conv_w.shape[1]
    xbc, glu, zs, dt = _hyb_in(h, g_pre, w_in, conv_ch, d_inner, ssm_conv_w, ssm_conv_b, ssm_dt_bias, seq)
    ya = _convmod(glu.reshape(bsz, seq, conv_ch), conv_dw_w, conv_dw_b, conv_ln_g, conv_ln_b)
    yb = _ssd(xbc.reshape(bsz, seq, xbc_w), zs.reshape(bsz, seq, d_inner), dt.reshape(bsz, seq, LANES),
              ssm_a_log, ssm_d, ssm_norm_g)
    t = bsz * seq
    return _proj_res(h, ya.reshape(t, conv_ch), yb.reshape(t, d_inner), w_out, g_post)


def _attention_mixer(h, bsz, seq, g_pre, g_post, w_qkv, w_o):
    n = w_qkv.shape[1]
    t, d = h.shape
    qkv = _qkv_proj(h, g_pre, w_qkv.astype(BF16))
    outs, lses = [], []
    for g in range(len(DILATED_PATTERNS)):
        o, lse = _attn_group(qkv, g, bsz, seq)
        outs.append(o)
        lses.append(lse)
    return _merge_proj(h, outs, lses, w_o, g_post)


def kernel(x, norm_g, ffn_w1, ffn_w2, hyb_w_in, conv_dw_w, conv_dw_b, conv_ln_g, conv_ln_b, ssm_conv_w, ssm_conv_b, ssm_dt_bias, ssm_a_log, ssm_d, ssm_norm_g, hyb_w_out, attn_w_qkv, attn_w_o):
    bsz, seq, d = x.shape
    h = x.reshape(bsz * seq, d)
    for i in range(norm_g.shape[0]):
        g = norm_g[i]
        j = i // 2
        h = _ffn(h, g[0], g[1], ffn_w1[i, 0], ffn_w2[i, 0])
        if i % 2 == 0:
            h = _conv_ssd_mixer(h, bsz, seq, g[2], g[3], hyb_w_in[j], conv_dw_w[j], conv_dw_b[j],
                                conv_ln_g[j], conv_ln_b[j], ssm_conv_w[j], ssm_conv_b[j], ssm_dt_bias[j],
                                ssm_a_log[j], ssm_d[j], ssm_norm_g[j], hyb_w_out[j])
        else:
            h = _attention_mixer(h, bsz, seq, g[2], g[3], attn_w_qkv[j], attn_w_o[j])
        h = _ffn(h, g[4], g[5], ffn_w1[i, 1], ffn_w2[i, 1])
    return h.reshape(bsz, seq, d)
```

```python
import functools

import jax
import jax.numpy as jnp
from jax import lax
from jax.experimental import pallas as pl
from jax.experimental.pallas import tpu as pltpu

NORM_EPS = 1e-6
CONV_WIDTH = 31
SSM_HEAD_DIM = 64
SSM_HEADS = 16
SSM_GROUPS = 2
SSM_STATE = 128
SSM_CONV_WIDTH = 4
SSM_CHUNK = 128
ATTN_HEAD_DIM = 64
ATTN_HEADS = 8
DILATED_PATTERNS = ((128, 1), (512, 4), (2048, 16))
ATTN_BLOCK = 128

LANES = 128
SUBLANES = 8
VMEM_LIMIT_BYTES = 56 * 1024 * 1024
MASK_VALUE = -1e30

F32 = jnp.float32
BF16 = jnp.bfloat16


def _params(*sem):
    return pltpu.CompilerParams(dimension_semantics=sem, vmem_limit_bytes=VMEM_LIMIT_BYTES)


def _resident(shape):
    nd = len(shape)
    return pl.BlockSpec(shape, lambda *_: (0,) * nd, pipeline_mode=pl.Buffered(1))


def _stacked(shape, which):
    lead = len(which)
    nd = len(shape)
    return pl.BlockSpec((None,) * lead + tuple(shape[lead:]), lambda *_: tuple(which) + (0,) * (nd - lead),
                        pipeline_mode=pl.Buffered(1))


def _rms(x, g):
    return x * lax.rsqrt(jnp.mean(x * x, axis=-1, keepdims=True) + NORM_EPS) * g


def _silu(x):
    return x * jax.nn.sigmoid(x)


def _mm(a, b):
    return jnp.dot(a, b, preferred_element_type=F32)


def _ffn_kernel(h_ref, gpre_ref, gpost_ref, w1_ref, w2_ref, o_ref):
    f = w2_ref.shape[0]
    h = h_ref[...]
    u = _rms(h, gpre_ref[...]).astype(BF16)
    gate = _mm(u, w1_ref[:, :f])
    up = _mm(u, w1_ref[:, f:])
    act = (_silu(gate) * up).astype(BF16)
    y = _mm(act, w2_ref[...])
    o_ref[...] = h + 0.5 * _rms(y, gpost_ref[...])


def _ffn(h, g_pre, g_post, w1, w2, which, tm=512):
    t, d = h.shape
    f = w2.shape[-2]
    row = pl.BlockSpec((tm, d), lambda i: (i, 0))
    return pl.pallas_call(
        _ffn_kernel,
        out_shape=jax.ShapeDtypeStruct((t, d), F32),
        grid=(t // tm,),
        in_specs=[row, _resident((1, d)), _resident((1, d)),
                  _stacked(w1.shape, which), _stacked(w2.shape, which)],
        out_specs=row,
        compiler_params=_params("parallel"),
        name="ffn",
    )(h, g_pre.reshape(1, d), g_post.reshape(1, d), w1, w2)


RESIDUES_PER_STEP = SUBLANES


def _gather_residues(h_ref):
    return jnp.concatenate([h_ref[:, r, :] for r in range(h_ref.shape[1])], axis=0)


def _qkv_kernel(h_ref, g_ref, w_ref, o_ref, u_ref):
    @pl.when(pl.program_id(1) == 0)
    def _():
        u_ref[...] = _rms(_gather_residues(h_ref), g_ref[...]).astype(BF16)

    n = o_ref.shape[1]
    col = pl.multiple_of(pl.program_id(1) * n, LANES)
    o_ref[...] = _mm(u_ref[...], w_ref[:, pl.ds(col, n)]).astype(o_ref.dtype)


def _qkv_proj(h, g, w, which, nsplit=3):
    t, d = h.shape
    n = w.shape[-1]
    rows = RESIDUES_PER_STEP * ATTN_BLOCK
    per_span = MAX_DILATION // RESIDUES_PER_STEP
    view = h.reshape(t // SPAN, ATTN_BLOCK, MAX_DILATION, d)
    return pl.pallas_call(
        _qkv_kernel,
        out_shape=jax.ShapeDtypeStruct((t, n), BF16),
        grid=(t // rows, nsplit),
        in_specs=[pl.BlockSpec((None, ATTN_BLOCK, RESIDUES_PER_STEP, d),
                               lambda i, j: (i // per_span, 0, i % per_span, 0)),
                  _resident((1, d)), _stacked(w.shape, which)],
        out_specs=pl.BlockSpec((rows, n // nsplit), lambda i, j: (i, j)),
        scratch_shapes=[pltpu.VMEM((rows, d), BF16)],
        compiler_params=_params("parallel", "arbitrary"),
        name="qkv_proj",
    )(view, g.reshape(1, d), w)


def _hyb_in_kernel(h_ref, g_ref, w_ref, cw_ref, cb_ref, dtb_ref,
                   xbc_ref, glu_ref, zs_ref, dt_ref, xbuf_ref, *, tiles_per_seq):
    tm = h_ref.shape[0]
    s0 = 2 * glu_ref.shape[1]
    s1 = s0 + zs_ref.shape[1]
    s2 = s1 + xbc_ref.shape[1]
    wg_ref, wz_ref, wx_ref, wd_ref = w_ref.at[:, :s0], w_ref.at[:, s0:s1], w_ref.at[:, s1:s2], w_ref.at[:, s2:]
    u = _rms(h_ref[...], g_ref[...]).astype(BF16)
    i = pl.program_id(0)

    @pl.when(i % tiles_per_seq == 0)
    def _():
        xbuf_ref[0:SUBLANES, :] = jnp.zeros((SUBLANES, xbuf_ref.shape[1]), F32)

    @pl.when(i % tiles_per_seq != 0)
    def _():
        xbuf_ref[0:SUBLANES, :] = xbuf_ref[tm:tm + SUBLANES, :]

    xbuf_ref[SUBLANES:SUBLANES + tm, :] = _mm(u, wx_ref[...])
    first = SUBLANES - (SSM_CONV_WIDTH - 1)
    acc = jnp.broadcast_to(cb_ref[...], (tm, xbuf_ref.shape[1]))
    for k in range(SSM_CONV_WIDTH):
        acc = acc + xbuf_ref[first + k:first + k + tm, :] * cw_ref[k:k + 1, :]
    xbc_ref[...] = _silu(acc)

    ch = glu_ref.shape[1]
    vg = _mm(u, wg_ref[...])
    glu_ref[...] = vg[:, :ch] * jax.nn.sigmoid(vg[:, ch:])
    zs_ref[...] = _silu(_mm(u, wz_ref[...]))
    dt_ref[...] = jax.nn.softplus(_mm(u, wd_ref[...]) + dtb_ref[...])


def _hyb_in(h, g, w_in, which, conv_ch, d_inner, ssm_conv_w, ssm_conv_b, dt_bias, seq, tm=512):
    t, d = h.shape
    xbc_w = ssm_conv_w.shape[1]
    assert w_in.shape[-1] == 2 * conv_ch + d_inner + xbc_w + LANES
    cw = jnp.zeros((SUBLANES, xbc_w), F32).at[:SSM_CONV_WIDTH].set(ssm_conv_w)
    row = lambda n: pl.BlockSpec((tm, n), lambda i: (i, 0))
    widths = (xbc_w, conv_ch, d_inner, LANES)
    return pl.pallas_call(
        functools.partial(_hyb_in_kernel, tiles_per_seq=seq // tm),
        out_shape=[jax.ShapeDtypeStruct((t, n), F32) for n in widths],
        grid=(t // tm,),
        in_specs=[row(d), _resident((1, d)), _stacked(w_in.shape, which),
                  _resident((SUBLANES, xbc_w)), _resident((1, xbc_w)), _resident((1, LANES))],
        out_specs=[row(n) for n in widths],
        scratch_shapes=[pltpu.VMEM((tm + SUBLANES, xbc_w), F32)],
        compiler_params=_params("arbitrary"),
        name="hyb_in",
    )(h, g.reshape(1, d), w_in, cw, ssm_conv_b.reshape(1, xbc_w),
      jnp.pad(dt_bias, (0, LANES - SSM_HEADS)).reshape(1, LANES))


CONV_TAIL = 32
CONV_ROW_SPLIT = 2


def _convmod_kernel(glu_ref, w_ref, b_ref, lg_ref, lb_ref, o_ref, buf_ref, hid_ref, *, tl):
    ch = glu_ref.shape[-1]
    nt = ch // LANES

    @pl.when(pl.program_id(1) == 0)
    def _():
        buf_ref[0, :, 0:CONV_TAIL, :] = jnp.zeros((nt, CONV_TAIL, LANES), F32)

    @pl.when(pl.program_id(1) > 0)
    def _():
        buf_ref[0, :, 0:CONV_TAIL, :] = buf_ref[0, :, tl:tl + CONV_TAIL, :]

    for c in range(nt):
        buf_ref[0, c, CONV_TAIL:CONV_TAIL + tl, :] = glu_ref[:, c * LANES:(c + 1) * LANES]

    first = CONV_TAIL - (CONV_WIDTH - 1)
    shifted_rows = tl + CONV_TAIL - SUBLANES
    rows = tl // CONV_ROW_SPLIT
    tiles = (rows // SUBLANES, SUBLANES, LANES)

    def lane_tile(c, carry):
        for r in range(1, SUBLANES):
            buf_ref[r, c, 0:shifted_rows, :] = buf_ref[0, c, r:r + shifted_rows, :]
        wc = w_ref[c]
        wk = [jnp.broadcast_to(wc[k:k + 1, :], (SUBLANES, LANES))[None] for k in range(CONV_WIDTH)]
        for part in range(CONV_ROW_SPLIT):
            acc = jnp.broadcast_to(b_ref[c], tiles)
            for k in range(CONV_WIDTH):
                shift = (first + k) % SUBLANES
                row = part * rows + first + k - shift
                acc = acc + buf_ref[shift, c, row:row + rows, :].reshape(tiles) * wk[k]
            hid_ref[c, part * rows:(part + 1) * rows, :] = acc.reshape(rows, LANES)
        return carry

    lax.fori_loop(0, nt, lane_tile, 0)

    hid = jnp.concatenate([hid_ref[c] for c in range(nt)], axis=1)
    xc = hid - jnp.mean(hid, axis=-1, keepdims=True)
    y = xc * lax.rsqrt(jnp.mean(xc * xc, axis=-1, keepdims=True) + NORM_EPS)
    o_ref[...] = _silu(y * lg_ref[...] + lb_ref[...]).astype(o_ref.dtype)


def _convmod(glu, w, b, ln_g, ln_b, tl=256):
    bsz, seq, ch = glu.shape
    nt = ch // LANES
    wpad = jnp.zeros((CONV_TAIL, ch), F32).at[:CONV_WIDTH].set(w)
    wpad = wpad.reshape(CONV_TAIL, nt, LANES).transpose(1, 0, 2)
    vec = _resident((1, ch))
    return pl.pallas_call(
        functools.partial(_convmod_kernel, tl=tl),
        out_shape=jax.ShapeDtypeStruct((bsz, seq, ch), BF16),
        grid=(bsz, seq // tl),
        in_specs=[pl.BlockSpec((None, tl, ch), lambda bi, i: (bi, i, 0)),
                  _resident((nt, CONV_TAIL, LANES)), _resident((nt, 1, LANES)), vec, vec],
        out_specs=pl.BlockSpec((None, tl, ch), lambda bi, i: (bi, i, 0)),
        scratch_shapes=[pltpu.VMEM((SUBLANES, nt, tl + CONV_TAIL, LANES), F32), pltpu.VMEM((nt, tl, LANES), F32)],
        compiler_params=_params("parallel", "arbitrary"),
        name="convmod",
    )(glu, wpad, b.reshape(nt, 1, LANES), ln_g.reshape(1, ch), ln_b.reshape(1, ch))


SSD_CHUNKS_PER_STEP = 4


def _ssd_kernel(xbc_ref, zs_ref, dt_ref, alog_ref, dskip_ref, ng_ref, expand_ref, tri_ref, o_ref, state_ref):
    q = SSM_CHUNK
    nchunk = xbc_ref.shape[0] // q
    d_inner = SSM_HEADS * SSM_HEAD_DIM
    gcols = d_inner // SSM_GROUPS
    nst = SSM_STATE
    chunk = lambda c: slice(c * q, (c + 1) * q)

    @pl.when(pl.program_id(1) == 0)
    def _():
        state_ref[...] = jnp.zeros(state_ref.shape, F32)

    dt = dt_ref[...]
    a = dt * (-jnp.exp(alog_ref[...]))
    tri = tri_ref[...]
    a_hi = a.astype(BF16)
    a_mid = (a - a_hi.astype(F32)).astype(BF16)
    a_lo = (a - a_hi.astype(F32) - a_mid.astype(F32)).astype(BF16)
    acums = [_mm(tri, a_hi[chunk(c)]) + _mm(tri, a_mid[chunk(c)]) + _mm(tri, a_lo[chunk(c)])
             for c in range(nchunk)]
    acum = jnp.concatenate(acums, axis=0)
    dstate = jnp.concatenate([jnp.exp(ac[q - 1:q, :] - ac) for ac in acums], axis=0)
    per_head = jnp.concatenate([jnp.exp(acum), dstate * dt], axis=0)
    ph_hi = per_head.astype(BF16)
    ph_lo = (per_head - ph_hi.astype(F32)).astype(BF16)
    full = _mm(ph_hi, expand_ref[...]) + _mm(ph_lo, expand_ref[...])
    rows = nchunk * q
    eacum_full = full[0:rows]
    xs = xbc_ref[:, :d_inner]
    xw = (xs * full[rows:2 * rows]).astype(BF16)
    causal = lax.broadcasted_iota(jnp.int32, (q, q), 0) >= lax.broadcasted_iota(jnp.int32, (q, q), 1)
    low_half = lax.broadcasted_iota(jnp.int32, (q, LANES), 1) < SSM_HEAD_DIM
    heads_per_group = SSM_HEADS // SSM_GROUPS

    state = state_ref[...]
    for c in range(nchunk):
        r = chunk(c)
        ac = acums[c]
        ac_t = ac.T
        dt_t = dt[r].T
        y_diag, y_off, new_states = [], [], []
        for g in range(SSM_GROUPS):
            bg = xbc_ref[r, d_inner + g * nst:d_inner + (g + 1) * nst].astype(BF16)
            cg = xbc_ref[r, d_inner + (SSM_GROUPS + g) * nst:d_inner + (SSM_GROUPS + g + 1) * nst].astype(BF16)
            gsl = slice(g * gcols, (g + 1) * gcols)
            cb = lax.dot_general(cg, bg, (((1,), (1,)), ((), ())), preferred_element_type=F32)
            y_off.append(_mm(cg, state[:, gsl].astype(BF16)))
            new_states.append(lax.dot_general(bg, xw[r, gsl], (((0,), (0,)), ((), ())),
                                              preferred_element_type=F32))
            for pair in range(heads_per_group // 2):
                h0 = g * heads_per_group + 2 * pair
                masks = []
                for h in (h0, h0 + 1):
                    diff = ac[:, h:h + 1] - ac_t[h:h + 1, :]
                    decay_in = jnp.exp(jnp.where(causal, diff, MASK_VALUE)) * dt_t[h:h + 1, :]
                    masks.append((cb * decay_in).astype(BF16))
                lhs = jnp.concatenate(masks, axis=1)
                xp = xs[r, h0 * SSM_HEAD_DIM:(h0 + 2) * SSM_HEAD_DIM]
                rhs = jnp.concatenate([jnp.where(low_half, xp, 0.0), jnp.where(low_half, 0.0, xp)],
                                      axis=0).astype(BF16)
                y_diag.append(_mm(lhs, rhs))
        decay = eacum_full[r]
        y = (jnp.concatenate(y_diag, axis=1) + jnp.concatenate(y_off, axis=1) * decay
             + dskip_ref[...] * xs[r])
        state = state * decay[q - 1:q, :] + jnp.concatenate(new_states, axis=1)
        y = y * zs_ref[r, :]
        o_ref[r, :] = _rms(y, ng_ref[...]).astype(o_ref.dtype)
    state_ref[...] = state


def _ssd(xbc, zs, dt, a_log, d_skip, norm_g):
    bsz, seq, xbc_w = xbc.shape
    q = SSM_CHUNK
    rows = SSD_CHUNKS_PER_STEP * q
    d_inner = SSM_HEADS * SSM_HEAD_DIM
    head_of_col = jnp.arange(d_inner) // SSM_HEAD_DIM
    expand = (jnp.arange(LANES)[:, None] == head_of_col[None, :]).astype(BF16)
    tri = (jnp.arange(q)[:, None] >= jnp.arange(q)[None, :]).astype(BF16)
    return pl.pallas_call(
        _ssd_kernel,
        out_shape=jax.ShapeDtypeStruct((bsz, seq, d_inner), BF16),
        grid=(bsz, seq // rows),
        in_specs=[pl.BlockSpec((None, rows, xbc_w), lambda bi, i: (bi, i, 0)),
                  pl.BlockSpec((None, rows, d_inner), lambda bi, i: (bi, i, 0)),
                  pl.BlockSpec((None, rows, LANES), lambda bi, i: (bi, i, 0)),
                  _resident((1, LANES)), _resident((1, d_inner)), _resident((1, d_inner)),
                  _resident((LANES, d_inner)), _resident((q, q))],
        out_specs=pl.BlockSpec((None, rows, d_inner), lambda bi, i: (bi, i, 0)),
        scratch_shapes=[pltpu.VMEM((SSM_STATE, d_inner), F32)],
        compiler_params=_params("parallel", "arbitrary"),
        name="ssd",
    )(xbc, zs, dt, jnp.pad(a_log, (0, LANES - SSM_HEADS)).reshape(1, LANES),
      jnp.repeat(d_skip, SSM_HEAD_DIM).reshape(1, d_inner), norm_g.reshape(1, d_inner), expand, tri)


def _proj_res_kernel(h_ref, xa_ref, xb_ref, w_ref, g_ref, o_ref):
    ka = xa_ref.shape[1]
    y = _mm(xa_ref[...], w_ref[:ka, :]) + _mm(xb_ref[...], w_ref[ka:, :])
    o_ref[...] = h_ref[...] + _rms(y, g_ref[...])


def _proj_res(h, xa, xb, w, which, g, tm=512):
    t, d = h.shape
    ka, kb = xa.shape[1], xb.shape[1]
    assert w.shape[-2] == ka + kb
    row = lambda n: pl.BlockSpec((tm, n), lambda i: (i, 0))
    return pl.pallas_call(
        _proj_res_kernel,
        out_shape=jax.ShapeDtypeStruct((t, d), F32),
        grid=(t // tm,),
        in_specs=[row(d), row(ka), row(kb), _stacked(w.shape, which), _resident((1, d))],
        out_specs=row(d),
        compiler_params=_params("parallel"),
        name="proj_res",
    )(h, xa, xb, w, g.reshape(1, d))


MAX_DILATION = 16
SPAN = ATTN_BLOCK * MAX_DILATION
ATTN_WIDTH = ATTN_HEADS * ATTN_HEAD_DIM


def _attend(jobs, bias_ref, scores_ref):
    dh = ATTN_HEAD_DIM
    nq = ATTN_BLOCK
    npairs = ATTN_HEADS // 2
    low_head = lax.broadcasted_iota(jnp.int32, (nq, LANES), 1) < dh
    for j, (q, k, _, _) in enumerate(jobs):
        for pair in range(npairs):
            cols = slice(pair * LANES, (pair + 1) * LANES)
            qp = q[:, cols]
            q2 = jnp.concatenate([jnp.where(low_head, qp, 0), jnp.where(low_head, 0, qp)], axis=0)
            scores_ref[j * npairs + pair] = lax.dot_general(k[:, cols], q2, (((1,), (1,)), ((), ())),
                                                            preferred_element_type=F32)
    results = []
    for j, (_, _, v, table) in enumerate(jobs):
        outs, lses = [], []
        for pair in range(npairs):
            cols = slice(pair * LANES, (pair + 1) * LANES)
            st = scores_ref[j * npairs + pair] + bias_ref[table, pair]
            m = jnp.max(st, axis=0, keepdims=True)
            p = jnp.exp(st - m)
            l = jnp.sum(p, axis=0, keepdims=True)
            ot = lax.dot_general(v[:, cols], p.astype(BF16), (((0,), (0,)), ((), ())), preferred_element_type=F32)
            ot = ot / l
            lse = m + jnp.log(l)
            for hh in range(2):
                outs.append(ot[hh * dh:(hh + 1) * dh, hh * nq:(hh + 1) * nq])
                lses.append(lse[:, hh * nq:(hh + 1) * nq])
        lse_t = jnp.concatenate(lses + [jnp.zeros((LANES - ATTN_HEADS, nq), F32)], axis=0)
        results.append((jnp.concatenate(outs, axis=0).T, lse_t.T))
    return results


BLOCKS_PER_STEP = 4


def _reset_carry(kprev_ref, vprev_ref):
    @pl.when(pl.program_id(2) == 0)
    def _():
        kprev_ref[...] = jnp.zeros(kprev_ref.shape, kprev_ref.dtype)
        vprev_ref[...] = jnp.zeros(vprev_ref.shape, vprev_ref.dtype)


def _attn_pairs_kernel(q_ref, kprev_ref, k_ref, vprev_ref, v_ref, bias_ref, o_ref, lse_ref, scores_ref):
    table = jnp.minimum(pl.program_id(2), 1)
    scale = ATTN_HEAD_DIM ** -0.5
    blocks = [slice(b * ATTN_BLOCK, (b + 1) * ATTN_BLOCK) for b in range(BLOCKS_PER_STEP)]
    jobs = [(q_ref[rows, :] * scale,
             jnp.concatenate([kprev_ref[rows, :], k_ref[rows, :]], axis=0),
             jnp.concatenate([vprev_ref[rows, :], v_ref[rows, :]], axis=0), table) for rows in blocks]
    for rows, (o, lse) in zip(blocks, _attend(jobs, bias_ref, scores_ref)):
        o_ref[rows, :] = o.astype(o_ref.dtype)
        lse_ref[rows, :] = lse


def _attn_chain_kernel(q_ref, k_ref, v_ref, bias_ref, o_ref, lse_ref, scores_ref, kprev_ref, vprev_ref, *, runs,
                       run_rows):
    _reset_carry(kprev_ref, vprev_ref)
    part = run_rows // BLOCKS_PER_STEP
    scale = ATTN_HEAD_DIM ** -0.5

    def parts(ref):
        x = ref[...].astype(F32)
        return [x[:, p * part:(p + 1) * part, :].reshape(ATTN_BLOCK, x.shape[-1])
                for p in range(BLOCKS_PER_STEP)]

    k32, v32 = parts(k_ref), parts(v_ref)
    qs = [x.astype(BF16) for x in parts(q_ref)]
    ks = [x.astype(BF16) for x in k32]
    vs = [x.astype(BF16) for x in v32]
    kprev = [kprev_ref[...].astype(BF16)] + ks[:-1]
    vprev = [vprev_ref[...].astype(BF16)] + vs[:-1]
    first_table = jnp.minimum(pl.program_id(2), 1)
    jobs = [(qs[p] * scale, jnp.concatenate([kprev[p], ks[p]], axis=0),
             jnp.concatenate([vprev[p], vs[p]], axis=0), first_table if p == 0 else 1)
            for p in range(BLOCKS_PER_STEP)]
    results = _attend(jobs, bias_ref, scores_ref)

    def interleave(xs):
        return jnp.concatenate([x.reshape(runs, part, x.shape[-1]) for x in xs], axis=1)

    o_ref[...] = interleave([o for o, _ in results]).astype(o_ref.dtype)
    lse_ref[...] = interleave([lse for _, lse in results])
    kprev_ref[...] = k32[-1]
    vprev_ref[...] = v32[-1]


def _attn_bias(window, dilation, offsets):
    steps = window // dilation
    slopes = jnp.exp2(-8.0 * jnp.arange(1, ATTN_HEADS + 1, dtype=F32) / ATTN_HEADS)
    kj = jnp.concatenate([offsets - ATTN_BLOCK, offsets])[:, None]
    qi = offsets[None, :]
    rel = qi - kj
    valid = (rel >= 0) & (rel <= steps)
    bias = -slopes[:, None, None] * (rel * dilation).astype(F32)
    later = jnp.where(valid[None], bias, MASK_VALUE)
    first = jnp.where((valid & (kj >= 0))[None], bias, MASK_VALUE)
    table = jnp.stack([first, later], axis=0)
    table = table.reshape(2, ATTN_HEADS // 2, 2, 2 * ATTN_BLOCK, ATTN_BLOCK)
    return table.transpose(0, 1, 3, 2, 4).reshape(2, ATTN_HEADS // 2, 2 * ATTN_BLOCK, 2 * ATTN_BLOCK)


def _attn_group(qkv, g, bsz, seq):
    window, dilation = DILATED_PATTERNS[g]
    ngroups = len(DILATED_PATTERNS)
    assert window // dilation <= ATTN_BLOCK and seq % SPAN == 0 and MAX_DILATION % dilation == 0
    t = bsz * seq
    nspan = seq // SPAN
    ncols = qkv.shape[1]
    col = lambda which: which * ngroups + g
    bias_spec = pl.BlockSpec((2, ATTN_HEADS // 2, 2 * ATTN_BLOCK, 2 * ATTN_BLOCK), lambda *_: (0, 0, 0, 0),
                             pipeline_mode=pl.Buffered(1))
    u = jnp.arange(ATTN_BLOCK)
    nb = BLOCKS_PER_STEP
    scores_scratch = pltpu.VMEM((nb * ATTN_HEADS // 2, 2 * ATTN_BLOCK, 2 * ATTN_BLOCK), F32)

    if dilation == MAX_DILATION:
        groups = MAX_DILATION // nb
        rows = nb * ATTN_BLOCK
        cur = lambda which: pl.BlockSpec((rows, ATTN_WIDTH),
                                         lambda b, r, j: ((b * nspan + j) * groups + r, col(which)))
        prev = lambda which: pl.BlockSpec(
            (rows, ATTN_WIDTH), lambda b, r, j: ((b * nspan + jnp.maximum(j - 1, 0)) * groups + r, col(which)))
        out = lambda w: pl.BlockSpec((rows, w), lambda b, r, j: ((b * nspan + j) * groups + r, 0))
        o, lse = pl.pallas_call(
            _attn_pairs_kernel,
            out_shape=[jax.ShapeDtypeStruct((t, ATTN_WIDTH), BF16), jax.ShapeDtypeStruct((t, LANES), F32)],
            grid=(bsz, groups, nspan),
            in_specs=[cur(0), prev(1), cur(1), prev(2), cur(2), bias_spec],
            out_specs=[out(ATTN_WIDTH), out(LANES)],
            scratch_shapes=[scores_scratch],
            compiler_params=_params("parallel", "parallel", "arbitrary"),
            name=f"attn_d{dilation}",
        )(qkv, qkv, qkv, qkv, qkv, _attn_bias(window, dilation, u))
        return o, lse

    runs = MAX_DILATION // dilation
    run_rows = nb * ATTN_BLOCK // runs
    steps_per_span = ATTN_BLOCK // run_rows
    shape = (bsz * nspan, runs, dilation, steps_per_span, run_rows)
    blk = lambda w: (None, runs, None, None, run_rows, w)
    at = lambda b, r, s, c: (b * nspan + s // steps_per_span, 0, r, s % steps_per_span, 0, c)
    cur = lambda which: pl.BlockSpec(blk(ATTN_WIDTH), lambda b, r, s: at(b, r, s, col(which)))
    out = lambda w: pl.BlockSpec(blk(w), lambda b, r, s: at(b, r, s, 0))
    view = qkv.reshape(shape + (ncols,))
    part = run_rows // nb
    offsets = runs * (u % part) + u // part
    carry = pltpu.VMEM((ATTN_BLOCK, ATTN_WIDTH), F32)
    o, lse = pl.pallas_call(
        functools.partial(_attn_chain_kernel, runs=runs, run_rows=run_rows),
        out_shape=[jax.ShapeDtypeStruct(shape + (ATTN_WIDTH,), BF16), jax.ShapeDtypeStruct(shape + (LANES,), F32)],
        grid=(bsz, dilation, nspan * steps_per_span),
        in_specs=[cur(0), cur(1), cur(2), bias_spec],
        out_specs=[out(ATTN_WIDTH), out(LANES)],
        scratch_shapes=[scores_scratch, carry, carry],
        compiler_params=_params("parallel", "parallel", "arbitrary"),
        name=f"attn_d{dilation}",
    )(view, view, view, _attn_bias(window, dilation, offsets))
    return o.reshape(t, ATTN_WIDTH), lse.reshape(t, LANES)


def _merge_proj_kernel(h_ref, o0_ref, o1_ref, o2_ref, l0_ref, l1_ref, l2_ref, expand_ref, w_ref, g_ref, out_ref):
    l0, l1, l2 = l0_ref[...], l1_ref[...], l2_ref[...]
    m = jnp.maximum(jnp.maximum(l0, l1), l2)
    e0, e1, e2 = jnp.exp(l0 - m), jnp.exp(l1 - m), jnp.exp(l2 - m)
    inv = 1.0 / (e0 + e1 + e2)
    merged = None
    for e, o_ref in ((e0, o0_ref), (e1, o1_ref), (e2, o2_ref)):
        wgt = e * inv
        w_hi = wgt.astype(BF16)
        w_lo = (wgt - w_hi.astype(F32)).astype(BF16)
        term = (_mm(w_hi, expand_ref[...]) + _mm(w_lo, expand_ref[...])) * o_ref[...]
        merged = term if merged is None else merged + term
    y = _mm(merged.astype(BF16), w_ref[...])
    new_h = _gather_residues(h_ref) + _rms(y, g_ref[...])
    for r in range(out_ref.shape[1]):
        out_ref[:, r, :] = new_h[r * ATTN_BLOCK:(r + 1) * ATTN_BLOCK, :]


def _merge_proj(h, outs, lses, w, which, g):
    t, d = h.shape
    k = w.shape[-2]
    rows = RESIDUES_PER_STEP * ATTN_BLOCK
    per_span = MAX_DILATION // RESIDUES_PER_STEP
    row = lambda n: pl.BlockSpec((rows, n), lambda i: (i, 0))
    tokens = pl.BlockSpec((None, ATTN_BLOCK, RESIDUES_PER_STEP, d), lambda i: (i // per_span, 0, i % per_span, 0))
    head_of_col = jnp.arange(k) // ATTN_HEAD_DIM
    expand = (jnp.arange(LANES)[:, None] == head_of_col[None, :]).astype(BF16)
    view = (t // SPAN, ATTN_BLOCK, MAX_DILATION, d)
    return pl.pallas_call(
        _merge_proj_kernel,
        out_shape=jax.ShapeDtypeStruct(view, F32),
        grid=(t // rows,),
        in_specs=[tokens] + [row(k)] * 3 + [row(LANES)] * 3
        + [_resident((LANES, k)), _stacked(w.shape, which), _resident((1, d))],
        out_specs=tokens,
        compiler_params=_params("parallel"),
        name="merge_proj",
    )(h.reshape(view), *outs, *lses, expand, w, g.reshape(1, d)).reshape(t, d)


def _conv_ssd_mixer(h, bsz, seq, g_pre, g_post, w_in, j, conv_dw_w, conv_dw_b, conv_ln_g, conv_ln_b,
                    ssm_conv_w, ssm_conv_b, ssm_dt_bias, ssm_a_log, ssm_d, ssm_norm_g, w_out):
    conv_ch = conv_dw_w.shape[1]
    d_inner = SSM_HEADS * SSM_HEAD_DIM
    xbc_w = ssm_conv_w.shape[1]
    xbc, glu, zs, dt = _hyb_in(h, g_pre, w_in, (j,), conv_ch, d_inner, ssm_conv_w, ssm_conv_b, ssm_dt_bias, seq)
    ya = _convmod(glu.reshape(bsz, seq, conv_ch), conv_dw_w, conv_dw_b, conv_ln_g, conv_ln_b)
    yb = _ssd(xbc.reshape(bsz, seq, xbc_w), zs.reshape(bsz, seq, d_inner), dt.reshape(bsz, seq, LANES),
              ssm_a_log, ssm_d, ssm_norm_g)
    t = bsz * seq
    return _proj_res(h, ya.reshape(t, conv_ch), yb.reshape(t, d_inner), w_out, (j,), g_post)


def _attention_mixer(h, bsz, seq, g_pre, g_post, w_qkv, w_o, j):
    qkv = _qkv_proj(h, g_pre, w_qkv, (j,))
    outs, lses = [], []
    for g in range(len(DILATED_PATTERNS)):
        o, lse = _attn_group(qkv, g, bsz, seq)
        outs.append(o)
        lses.append(lse)
    return _merge_proj(h, outs, lses, w_o, (j,), g_post)


def kernel(x, norm_g, ffn_w1, ffn_w2, hyb_w_in, conv_dw_w, conv_dw_b, conv_ln_g, conv_ln_b, ssm_conv_w, ssm_conv_b, ssm_dt_bias, ssm_a_log, ssm_d, ssm_norm_g, hyb_w_out, attn_w_qkv, attn_w_o):
    bsz, seq, d = x.shape
    h = x.reshape(bsz * seq, d)
    w1, w2 = ffn_w1.astype(BF16), ffn_w2.astype(BF16)
    w_in = jnp.pad(hyb_w_in, ((0, 0), (0, 0), (0, LANES - SSM_HEADS))).astype(BF16)
    w_out, w_qkv, w_o = hyb_w_out.astype(BF16), attn_w_qkv.astype(BF16), attn_w_o.astype(BF16)
    for i in range(norm_g.shape[0]):
        g = norm_g[i]
        j = i // 2
        h = _ffn(h, g[0], g[1], w1, w2, (i, 0))
        if i % 2 == 0:
            h = _conv_ssd_mixer(h, bsz, seq, g[2], g[3], w_in, j, conv_dw_w[j], conv_dw_b[j],
                                conv_ln_g[j], conv_ln_b[j], ssm_conv_w[j], ssm_conv_b[j], ssm_dt_bias[j],
                                ssm_a_log[j], ssm_d[j], ssm_norm_g[j], w_out)
        else:
            h = _attention_mixer(h, bsz, seq, g[2], g[3], w_qkv, w_o, j)
        h = _ffn(h, g[4], g[5], w1, w2, (i, 1))
    return h.reshape(bsz, seq, d)
```

```python
import functools

import jax
import jax.numpy as jnp
from jax import lax
from jax.experimental import pallas as pl
from jax.experimental.pallas import tpu as pltpu

NORM_EPS = 1e-6
CONV_WIDTH = 31
SSM_HEAD_DIM = 64
SSM_HEADS = 16
SSM_GROUPS = 2
SSM_STATE = 128
SSM_CONV_WIDTH = 4
SSM_CHUNK = 128
ATTN_HEAD_DIM = 64
ATTN_HEADS = 8
DILATED_PATTERNS = ((128, 1), (512, 4), (2048, 16))
ATTN_BLOCK = 128

LANES = 128
SUBLANES = 8
VMEM_LIMIT_BYTES = 56 * 1024 * 1024
MASK_VALUE = -1e30

F32 = jnp.float32
BF16 = jnp.bfloat16


def _params(*sem):
    return pltpu.CompilerParams(dimension_semantics=sem, vmem_limit_bytes=VMEM_LIMIT_BYTES)


def _resident(shape):
    nd = len(shape)
    return pl.BlockSpec(shape, lambda *_: (0,) * nd, pipeline_mode=pl.Buffered(1))


def _stacked(shape, which):
    lead = len(which)
    nd = len(shape)
    return pl.BlockSpec((None,) * lead + tuple(shape[lead:]), lambda *_: tuple(which) + (0,) * (nd - lead),
                        pipeline_mode=pl.Buffered(1))


def _rms(x, g):
    return x * lax.rsqrt(jnp.mean(x * x, axis=-1, keepdims=True) + NORM_EPS) * g


def _silu(x):
    return x * jax.nn.sigmoid(x)


def _mm(a, b):
    return jnp.dot(a, b, preferred_element_type=F32)


FFN_ROW_GROUPS = 4


def _ffn_kernel(h_ref, gpre_ref, gpost_ref, w1_ref, w2_ref, o_ref):
    f = w2_ref.shape[0]
    half = h_ref.shape[0] // FFN_ROW_GROUPS
    for part in range(FFN_ROW_GROUPS):
        rows = slice(part * half, (part + 1) * half)
        h = h_ref[rows, :]
        u = _rms(h, gpre_ref[...]).astype(BF16)
        gate = _mm(u, w1_ref[:, :f])
        up = _mm(u, w1_ref[:, f:])
        act = (_silu(gate) * up).astype(BF16)
        y = _mm(act, w2_ref[...])
        o_ref[rows, :] = h + 0.5 * _rms(y, gpost_ref[...])


def _ffn(h, g_pre, g_post, w1, w2, which, tm=512):
    t, d = h.shape
    f = w2.shape[-2]
    row = pl.BlockSpec((tm, d), lambda i: (i, 0))
    return pl.pallas_call(
        _ffn_kernel,
        out_shape=jax.ShapeDtypeStruct((t, d), F32),
        grid=(t // tm,),
        in_specs=[row, _resident((1, d)), _resident((1, d)),
                  _stacked(w1.shape, which), _stacked(w2.shape, which)],
        out_specs=row,
        compiler_params=_params("parallel"),
        name="ffn",
    )(h, g_pre.reshape(1, d), g_post.reshape(1, d), w1, w2)


RESIDUES_PER_STEP = SUBLANES


RESIDUES_PER_GROUP = 2


def _gather_residues(h_ref, r0, count):
    return jnp.concatenate([h_ref[:, r, :] for r in range(r0, r0 + count)], axis=0)


def _qkv_kernel(h_ref, g_ref, w_ref, o_ref):
    group_rows = RESIDUES_PER_GROUP * ATTN_BLOCK
    for part in range(RESIDUES_PER_STEP // RESIDUES_PER_GROUP):
        u = _rms(_gather_residues(h_ref, part * RESIDUES_PER_GROUP, RESIDUES_PER_GROUP), g_ref[...]).astype(BF16)
        o_ref[part * group_rows:(part + 1) * group_rows, :] = _mm(u, w_ref[...]).astype(o_ref.dtype)


def _qkv_proj(h, g, w, which):
    t, d = h.shape
    n = w.shape[-1]
    rows = RESIDUES_PER_STEP * ATTN_BLOCK
    per_span = MAX_DILATION // RESIDUES_PER_STEP
    view = h.reshape(t // SPAN, ATTN_BLOCK, MAX_DILATION, d)
    return pl.pallas_call(
        _qkv_kernel,
        out_shape=jax.ShapeDtypeStruct((t, n), BF16),
        grid=(t // rows,),
        in_specs=[pl.BlockSpec((None, ATTN_BLOCK, RESIDUES_PER_STEP, d),
                               lambda i: (i // per_span, 0, i % per_span, 0)),
                  _resident((1, d)), _stacked(w.shape, which)],
        out_specs=pl.BlockSpec((rows, n), lambda i: (i, 0)),
        compiler_params=_params("parallel"),
        name="qkv_proj",
    )(view, g.reshape(1, d), w)


HYB_IN_ROW_GROUPS = 4


def _hyb_in_kernel(h_ref, g_ref, w_ref, cw_ref, cb_ref, dtb_ref,
                   xbc_ref, glu_ref, zs_ref, dt_ref, xbuf_ref, *, tiles_per_seq):
    tm = h_ref.shape[0]
    s0 = 2 * glu_ref.shape[1]
    s1 = s0 + zs_ref.shape[1]
    s2 = s1 + xbc_ref.shape[1]
    wg_ref, wz_ref, wx_ref, wd_ref = w_ref.at[:, :s0], w_ref.at[:, s0:s1], w_ref.at[:, s1:s2], w_ref.at[:, s2:]
    i = pl.program_id(0)

    @pl.when(i % tiles_per_seq == 0)
    def _():
        xbuf_ref[0:SUBLANES, :] = jnp.zeros((SUBLANES, xbuf_ref.shape[1]), F32)

    @pl.when(i % tiles_per_seq != 0)
    def _():
        xbuf_ref[0:SUBLANES, :] = xbuf_ref[tm:tm + SUBLANES, :]

    rows_per_group = tm // HYB_IN_ROW_GROUPS
    first = SUBLANES - (SSM_CONV_WIDTH - 1)
    ch = glu_ref.shape[1]
    for part in range(HYB_IN_ROW_GROUPS):
        r0 = part * rows_per_group
        rows = slice(r0, r0 + rows_per_group)
        u = _rms(h_ref[rows, :], g_ref[...]).astype(BF16)
        xbuf_ref[SUBLANES + r0:SUBLANES + r0 + rows_per_group, :] = _mm(u, wx_ref[...])
        acc = jnp.broadcast_to(cb_ref[...], (rows_per_group, xbuf_ref.shape[1]))
        for k in range(SSM_CONV_WIDTH):
            acc = acc + xbuf_ref[r0 + first + k:r0 + first + k + rows_per_group, :] * cw_ref[k:k + 1, :]
        xbc_ref[rows, :] = _silu(acc)
        vg = _mm(u, wg_ref[...])
        glu_ref[rows, :] = vg[:, :ch] * jax.nn.sigmoid(vg[:, ch:])
        zs_ref[rows, :] = _silu(_mm(u, wz_ref[...]))
        dt_ref[rows, :] = jax.nn.softplus(_mm(u, wd_ref[...]) + dtb_ref[...])


def _hyb_in(h, g, w_in, which, conv_ch, d_inner, ssm_conv_w, ssm_conv_b, dt_bias, seq, tm=512):
    t, d = h.shape
    xbc_w = ssm_conv_w.shape[1]
    assert w_in.shape[-1] == 2 * conv_ch + d_inner + xbc_w + LANES
    cw = jnp.zeros((SUBLANES, xbc_w), F32).at[:SSM_CONV_WIDTH].set(ssm_conv_w)
    row = lambda n: pl.BlockSpec((tm, n), lambda i: (i, 0))
    widths = (xbc_w, conv_ch, d_inner, LANES)
    return pl.pallas_call(
        functools.partial(_hyb_in_kernel, tiles_per_seq=seq // tm),
        out_shape=[jax.ShapeDtypeStruct((t, n), F32) for n in widths],
        grid=(t // tm,),
        in_specs=[row(d), _resident((1, d)), _stacked(w_in.shape, which),
                  _resident((SUBLANES, xbc_w)), _resident((1, xbc_w)), _resident((1, LANES))],
        out_specs=[row(n) for n in widths],
        scratch_shapes=[pltpu.VMEM((tm + SUBLANES, xbc_w), F32)],
        compiler_params=_params("arbitrary"),
        name="hyb_in",
    )(h, g.reshape(1, d), w_in, cw, ssm_conv_b.reshape(1, xbc_w),
      jnp.pad(dt_bias, (0, LANES - SSM_HEADS)).reshape(1, LANES))


CONV_TAIL = 32
CONV_ROW_SPLIT = 4


def _convmod_kernel(glu_ref, w_ref, b_ref, lg_ref, lb_ref, o_ref, buf_ref, hid_ref, *, tl):
    ch = glu_ref.shape[-1]
    nt = ch // LANES

    @pl.when(pl.program_id(1) == 0)
    def _():
        buf_ref[0, :, 0:CONV_TAIL, :] = jnp.zeros((nt, CONV_TAIL, LANES), F32)

    @pl.when(pl.program_id(1) > 0)
    def _():
        buf_ref[0, :, 0:CONV_TAIL, :] = buf_ref[0, :, tl:tl + CONV_TAIL, :]

    for c in range(nt):
        buf_ref[0, c, CONV_TAIL:CONV_TAIL + tl, :] = glu_ref[:, c * LANES:(c + 1) * LANES]

    first = CONV_TAIL - (CONV_WIDTH - 1)
    shifted_rows = tl + CONV_TAIL - SUBLANES
    rows = tl // CONV_ROW_SPLIT
    tiles = (rows // SUBLANES, SUBLANES, LANES)

    def lane_tile(c, carry):
        for r in range(1, SUBLANES):
            buf_ref[r, c, 0:shifted_rows, :] = buf_ref[0, c, r:r + shifted_rows, :]
        wc = w_ref[c]
        wk = [jnp.broadcast_to(wc[k:k + 1, :], (SUBLANES, LANES))[None] for k in range(CONV_WIDTH)]
        for part in range(CONV_ROW_SPLIT):
            acc = jnp.broadcast_to(b_ref[c], tiles)
            for k in range(CONV_WIDTH):
                shift = (first + k) % SUBLANES
                row = part * rows + first + k - shift
                acc = acc + buf_ref[shift, c, row:row + rows, :].reshape(tiles) * wk[k]
            hid_ref[c, part * rows:(part + 1) * rows, :] = acc.reshape(rows, LANES)
        return carry

    lax.fori_loop(0, nt, lane_tile, 0)

    hid = jnp.concatenate([hid_ref[c] for c in range(nt)], axis=1)
    xc = hid - jnp.mean(hid, axis=-1, keepdims=True)
    y = xc * lax.rsqrt(jnp.mean(xc * xc, axis=-1, keepdims=True) + NORM_EPS)
    o_ref[...] = _silu(y * lg_ref[...] + lb_ref[...]).astype(o_ref.dtype)


def _convmod(glu, w, b, ln_g, ln_b, tl=512):
    bsz, seq, ch = glu.shape
    nt = ch // LANES
    wpad = jnp.zeros((CONV_TAIL, ch), F32).at[:CONV_WIDTH].set(w)
    wpad = wpad.reshape(CONV_TAIL, nt, LANES).transpose(1, 0, 2)
    vec = _resident((1, ch))
    return pl.pallas_call(
        functools.partial(_convmod_kernel, tl=tl),
        out_shape=jax.ShapeDtypeStruct((bsz, seq, ch), BF16),
        grid=(bsz, seq // tl),
        in_specs=[pl.BlockSpec((None, tl, ch), lambda bi, i: (bi, i, 0)),
                  _resident((nt, CONV_TAIL, LANES)), _resident((nt, 1, LANES)), vec, vec],
        out_specs=pl.BlockSpec((None, tl, ch), lambda bi, i: (bi, i, 0)),
        scratch_shapes=[pltpu.VMEM((SUBLANES, nt, tl + CONV_TAIL, LANES), F32), pltpu.VMEM((nt, tl, LANES), F32)],
        compiler_params=_params("parallel", "arbitrary"),
        name="convmod",
    )(glu, wpad, b.reshape(nt, 1, LANES), ln_g.reshape(1, ch), ln_b.reshape(1, ch))


SSD_CHUNKS_PER_STEP = 8


def _ssd_kernel(xbc_ref, zs_ref, dt_ref, alog_ref, dskip_ref, ng_ref, expand_ref, tri_ref, o_ref, state_ref):
    q = SSM_CHUNK
    nchunk = xbc_ref.shape[0] // q
    d_inner = SSM_HEADS * SSM_HEAD_DIM
    gcols = d_inner // SSM_GROUPS
    nst = SSM_STATE
    chunk = lambda c: slice(c * q, (c + 1) * q)

    @pl.when(pl.program_id(1) == 0)
    def _():
        state_ref[...] = jnp.zeros(state_ref.shape, F32)

    dt = dt_ref[...]
    a = dt * (-jnp.exp(alog_ref[...]))
    tri = tri_ref[...]
    a_hi = a.astype(BF16)
    a_mid = (a - a_hi.astype(F32)).astype(BF16)
    a_lo = (a - a_hi.astype(F32) - a_mid.astype(F32)).astype(BF16)
    acums = [_mm(tri, a_hi[chunk(c)]) + _mm(tri, a_mid[chunk(c)]) + _mm(tri, a_lo[chunk(c)])
             for c in range(nchunk)]
    acum = jnp.concatenate(acums, axis=0)
    dstate = jnp.concatenate([jnp.exp(ac[q - 1:q, :] - ac) for ac in acums], axis=0)
    per_head = jnp.concatenate([jnp.exp(acum), dstate * dt], axis=0)
    ph_hi = per_head.astype(BF16)
    ph_lo = (per_head - ph_hi.astype(F32)).astype(BF16)
    full = _mm(ph_hi, expand_ref[...]) + _mm(ph_lo, expand_ref[...])
    rows = nchunk * q
    eacum_full = full[0:rows]
    xs = xbc_ref[:, :d_inner]
    xw = (xs * full[rows:2 * rows]).astype(BF16)
    causal = lax.broadcasted_iota(jnp.int32, (q, q), 0) >= lax.broadcasted_iota(jnp.int32, (q, q), 1)
    low_half = lax.broadcasted_iota(jnp.int32, (q, LANES), 1) < SSM_HEAD_DIM
    heads_per_group = SSM_HEADS // SSM_GROUPS

    state = state_ref[...]
    for c in range(nchunk):
        r = chunk(c)
        ac = acums[c]
        ac_t = ac.T
        dt_t = dt[r].T
        y_diag, y_off, new_states = [], [], []
        for g in range(SSM_GROUPS):
            bg = xbc_ref[r, d_inner + g * nst:d_inner + (g + 1) * nst].astype(BF16)
            cg = xbc_ref[r, d_inner + (SSM_GROUPS + g) * nst:d_inner + (SSM_GROUPS + g + 1) * nst].astype(BF16)
            gsl = slice(g * gcols, (g + 1) * gcols)
            cb = lax.dot_general(cg, bg, (((1,), (1,)), ((), ())), preferred_element_type=F32)
            y_off.append(_mm(cg, state[:, gsl].astype(BF16)))
            new_states.append(lax.dot_general(bg, xw[r, gsl], (((0,), (0,)), ((), ())),
                                              preferred_element_type=F32))
            for pair in range(heads_per_group // 2):
                h0 = g * heads_per_group + 2 * pair
                masks = []
                for h in (h0, h0 + 1):
                    diff = ac[:, h:h + 1] - ac_t[h:h + 1, :]
                    decay_in = jnp.exp(jnp.where(causal, diff, MASK_VALUE)) * dt_t[h:h + 1, :]
                    masks.append((cb * decay_in).astype(BF16))
                lhs = jnp.concatenate(masks, axis=1)
                xp = xs[r, h0 * SSM_HEAD_DIM:(h0 + 2) * SSM_HEAD_DIM]
                rhs = jnp.concatenate([jnp.where(low_half, xp, 0.0), jnp.where(low_half, 0.0, xp)],
                                      axis=0).astype(BF16)
                y_diag.append(_mm(lhs, rhs))
        decay = eacum_full[r]
        y = (jnp.concatenate(y_diag, axis=1) + jnp.concatenate(y_off, axis=1) * decay
             + dskip_ref[...] * xs[r])
        state = state * decay[q - 1:q, :] + jnp.concatenate(new_states, axis=1)
        y = y * zs_ref[r, :]
        o_ref[r, :] = _rms(y, ng_ref[...]).astype(o_ref.dtype)
    state_ref[...] = state


def _ssd(xbc, zs, dt, a_log, d_skip, norm_g):
    bsz, seq, xbc_w = xbc.shape
    q = SSM_CHUNK
    rows = SSD_CHUNKS_PER_STEP * q
    d_inner = SSM_HEADS * SSM_HEAD_DIM
    head_of_col = jnp.arange(d_inner) // SSM_HEAD_DIM
    expand = (jnp.arange(LANES)[:, None] == head_of_col[None, :]).astype(BF16)
    tri = (jnp.arange(q)[:, None] >= jnp.arange(q)[None, :]).astype(BF16)
    return pl.pallas_call(
        _ssd_kernel,
        out_shape=jax.ShapeDtypeStruct((bsz, seq, d_inner), BF16),
        grid=(bsz, seq // rows),
        in_specs=[pl.BlockSpec((None, rows, xbc_w), lambda bi, i: (bi, i, 0)),
                  pl.BlockSpec((None, rows, d_inner), lambda bi, i: (bi, i, 0)),
                  pl.BlockSpec((None, rows, LANES), lambda bi, i: (bi, i, 0)),
                  _resident((1, LANES)), _resident((1, d_inner)), _resident((1, d_inner)),
                  _resident((LANES, d_inner)), _resident((q, q))],
        out_specs=pl.BlockSpec((None, rows, d_inner), lambda bi, i: (bi, i, 0)),
        scratch_shapes=[pltpu.VMEM((SSM_STATE, d_inner), F32)],
        compiler_params=_params("parallel", "arbitrary"),
        name="ssd",
    )(xbc, zs, dt, jnp.pad(a_log, (0, LANES - SSM_HEADS)).reshape(1, LANES),
      jnp.repeat(d_skip, SSM_HEAD_DIM).reshape(1, d_inner), norm_g.reshape(1, d_inner), expand, tri)


def _proj_res_kernel(h_ref, xa_ref, xb_ref, w_ref, g_ref, o_ref):
    ka = xa_ref.shape[1]
    y = _mm(xa_ref[...], w_ref[:ka, :]) + _mm(xb_ref[...], w_ref[ka:, :])
    o_ref[...] = h_ref[...] + _rms(y, g_ref[...])


def _proj_res(h, xa, xb, w, which, g, tm=512):
    t, d = h.shape
    ka, kb = xa.shape[1], xb.shape[1]
    assert w.shape[-2] == ka + kb
    row = lambda n: pl.BlockSpec((tm, n), lambda i: (i, 0))
    return pl.pallas_call(
        _proj_res_kernel,
        out_shape=jax.ShapeDtypeStruct((t, d), F32),
        grid=(t // tm,),
        in_specs=[row(d), row(ka), row(kb), _stacked(w.shape, which), _resident((1, d))],
        out_specs=row(d),
        compiler_params=_params("parallel"),
        name="proj_res",
    )(h, xa, xb, w, g.reshape(1, d))


MAX_DILATION = 16
SPAN = ATTN_BLOCK * MAX_DILATION
ATTN_WIDTH = ATTN_HEADS * ATTN_HEAD_DIM


def _attend(jobs, bias_ref, scores_ref):
    dh = ATTN_HEAD_DIM
    nq = ATTN_BLOCK
    npairs = ATTN_HEADS // 2
    low_head = lax.broadcasted_iota(jnp.int32, (nq, LANES), 1) < dh
    for j, (q, k, _, _) in enumerate(jobs):
        for pair in range(npairs):
            cols = slice(pair * LANES, (pair + 1) * LANES)
            qp = q[:, cols]
            q2 = jnp.concatenate([jnp.where(low_head, qp, 0), jnp.where(low_head, 0, qp)], axis=0)
            scores_ref[j * npairs + pair] = lax.dot_general(k[:, cols], q2, (((1,), (1,)), ((), ())),
                                                            preferred_element_type=F32)
    results = []
    for j, (_, _, v, table) in enumerate(jobs):
        outs, lses = [], []
        for pair in range(npairs):
            cols = slice(pair * LANES, (pair + 1) * LANES)
            st = scores_ref[j * npairs + pair] + bias_ref[table, pair]
            m = jnp.max(st, axis=0, keepdims=True)
            p = jnp.exp(st - m)
            l = jnp.sum(p, axis=0, keepdims=True)
            ot = lax.dot_general(v[:, cols], p.astype(BF16), (((0,), (0,)), ((), ())), preferred_element_type=F32)
            ot = ot / l
            lse = m + jnp.log(l)
            for hh in range(2):
                outs.append(ot[hh * dh:(hh + 1) * dh, hh * nq:(hh + 1) * nq])
                lses.append(lse[:, hh * nq:(hh + 1) * nq])
        lse_t = jnp.concatenate(lses + [jnp.zeros((LANES - ATTN_HEADS, nq), F32)], axis=0)
        results.append((jnp.concatenate(outs, axis=0).T, lse_t.T))
    return results


BLOCKS_PER_STEP = 4


def _reset_carry(kprev_ref, vprev_ref):
    @pl.when(pl.program_id(2) == 0)
    def _():
        kprev_ref[...] = jnp.zeros(kprev_ref.shape, kprev_ref.dtype)
        vprev_ref[...] = jnp.zeros(vprev_ref.shape, vprev_ref.dtype)


def _attn_pairs_kernel(q_ref, kprev_ref, k_ref, vprev_ref, v_ref, bias_ref, o_ref, lse_ref, scores_ref):
    table = jnp.minimum(pl.program_id(2), 1)
    scale = ATTN_HEAD_DIM ** -0.5
    blocks = [slice(b * ATTN_BLOCK, (b + 1) * ATTN_BLOCK) for b in range(BLOCKS_PER_STEP)]
    jobs = [(q_ref[rows, :] * scale,
             jnp.concatenate([kprev_ref[rows, :], k_ref[rows, :]], axis=0),
             jnp.concatenate([vprev_ref[rows, :], v_ref[rows, :]], axis=0), table) for rows in blocks]
    for rows, (o, lse) in zip(blocks, _attend(jobs, bias_ref, scores_ref)):
        o_ref[rows, :] = o.astype(o_ref.dtype)
        lse_ref[rows, :] = lse


def _attn_chain_kernel(q_ref, k_ref, v_ref, bias_ref, o_ref, lse_ref, scores_ref, kprev_ref, vprev_ref, *, runs,
                       run_rows):
    _reset_carry(kprev_ref, vprev_ref)
    part = run_rows // BLOCKS_PER_STEP
    scale = ATTN_HEAD_DIM ** -0.5

    def parts(ref):
        x = ref[...].astype(F32)
        return [x[:, p * part:(p + 1) * part, :].reshape(ATTN_BLOCK, x.shape[-1])
                for p in range(BLOCKS_PER_STEP)]

    k32, v32 = parts(k_ref), parts(v_ref)
    qs = [x.astype(BF16) for x in parts(q_ref)]
    ks = [x.astype(BF16) for x in k32]
    vs = [x.astype(BF16) for x in v32]
    kprev = [kprev_ref[...].astype(BF16)] + ks[:-1]
    vprev = [vprev_ref[...].astype(BF16)] + vs[:-1]
    first_table = jnp.minimum(pl.program_id(2), 1)
    jobs = [(qs[p] * scale, jnp.concatenate([kprev[p], ks[p]], axis=0),
             jnp.concatenate([vprev[p], vs[p]], axis=0), first_table if p == 0 else 1)
            for p in range(BLOCKS_PER_STEP)]
    results = _attend(jobs, bias_ref, scores_ref)

    def interleave(xs):
        return jnp.concatenate([x.reshape(runs, part, x.shape[-1]) for x in xs], axis=1)

    o_ref[...] = interleave([o for o, _ in results]).astype(o_ref.dtype)
    lse_ref[...] = interleave([lse for _, lse in results])
    kprev_ref[...] = k32[-1]
    vprev_ref[...] = v32[-1]


def _attn_bias(window, dilation, offsets):
    steps = window // dilation
    slopes = jnp.exp2(-8.0 * jnp.arange(1, ATTN_HEADS + 1, dtype=F32) / ATTN_HEADS)
    kj = jnp.concatenate([offsets - ATTN_BLOCK, offsets])[:, None]
    qi = offsets[None, :]
    rel = qi - kj
    valid = (rel >= 0) & (rel <= steps)
    bias = -slopes[:, None, None] * (rel * dilation).astype(F32)
    later = jnp.where(valid[None], bias, MASK_VALUE)
    first = jnp.where((valid & (kj >= 0))[None], bias, MASK_VALUE)
    table = jnp.stack([first, later], axis=0)
    table = table.reshape(2, ATTN_HEADS // 2, 2, 2 * ATTN_BLOCK, ATTN_BLOCK)
    return table.transpose(0, 1, 3, 2, 4).reshape(2, ATTN_HEADS // 2, 2 * ATTN_BLOCK, 2 * ATTN_BLOCK)


def _attn_group(qkv, g, bsz, seq):
    window, dilation = DILATED_PATTERNS[g]
    ngroups = len(DILATED_PATTERNS)
    assert window // dilation <= ATTN_BLOCK and seq % SPAN == 0 and MAX_DILATION % dilation == 0
    t = bsz * seq
    nspan = seq // SPAN
    ncols = qkv.shape[1]
    col = lambda which: which * ngroups + g
    bias_spec = pl.BlockSpec((2, ATTN_HEADS // 2, 2 * ATTN_BLOCK, 2 * ATTN_BLOCK), lambda *_: (0, 0, 0, 0),
                             pipeline_mode=pl.Buffered(1))
    u = jnp.arange(ATTN_BLOCK)
    nb = BLOCKS_PER_STEP
    scores_scratch = pltpu.VMEM((nb * ATTN_HEADS // 2, 2 * ATTN_BLOCK, 2 * ATTN_BLOCK), F32)

    if dilation == MAX_DILATION:
        groups = MAX_DILATION // nb
        rows = nb * ATTN_BLOCK
        cur = lambda which: pl.BlockSpec((rows, ATTN_WIDTH),
                                         lambda b, r, j: ((b * nspan + j) * groups + r, col(which)))
        prev = lambda which: pl.BlockSpec(
            (rows, ATTN_WIDTH), lambda b, r, j: ((b * nspan + jnp.maximum(j - 1, 0)) * groups + r, col(which)))
        out = lambda w: pl.BlockSpec((rows, w), lambda b, r, j: ((b * nspan + j) * groups + r, 0))
        o, lse = pl.pallas_call(
            _attn_pairs_kernel,
            out_shape=[jax.ShapeDtypeStruct((t, ATTN_WIDTH), BF16), jax.ShapeDtypeStruct((t, LANES), F32)],
            grid=(bsz, groups, nspan),
            in_specs=[cur(0), prev(1), cur(1), prev(2), cur(2), bias_spec],
            out_specs=[out(ATTN_WIDTH), out(LANES)],
            scratch_shapes=[scores_scratch],
            compiler_params=_params("parallel", "parallel", "arbitrary"),
            name=f"attn_d{dilation}",
        )(qkv, qkv, qkv, qkv, qkv, _attn_bias(window, dilation, u))
        return o, lse

    runs = MAX_DILATION // dilation
    run_rows = nb * ATTN_BLOCK // runs
    steps_per_span = ATTN_BLOCK // run_rows
    shape = (bsz * nspan, runs, dilation, steps_per_span, run_rows)
    blk = lambda w: (None, runs, None, None, run_rows, w)
    at = lambda b, r, s, c: (b * nspan + s // steps_per_span, 0, r, s % steps_per_span, 0, c)
    cur = lambda which: pl.BlockSpec(blk(ATTN_WIDTH), lambda b, r, s: at(b, r, s, col(which)))
    out = lambda w: pl.BlockSpec(blk(w), lambda b, r, s: at(b, r, s, 0))
    view = qkv.reshape(shape + (ncols,))
    part = run_rows // nb
    offsets = runs * (u % part) + u // part
    carry = pltpu.VMEM((ATTN_BLOCK, ATTN_WIDTH), F32)
    o, lse = pl.pallas_call(
        functools.partial(_attn_chain_kernel, runs=runs, run_rows=run_rows),
        out_shape=[jax.ShapeDtypeStruct(shape + (ATTN_WIDTH,), BF16), jax.ShapeDtypeStruct(shape + (LANES,), F32)],
        grid=(bsz, dilation, nspan * steps_per_span),
        in_specs=[cur(0), cur(1), cur(2), bias_spec],
        out_specs=[out(ATTN_WIDTH), out(LANES)],
        scratch_shapes=[scores_scratch, carry, carry],
        compiler_params=_params("parallel", "parallel", "arbitrary"),
        name=f"attn_d{dilation}",
    )(view, view, view, _attn_bias(window, dilation, offsets))
    return o.reshape(t, ATTN_WIDTH), lse.reshape(t, LANES)


def _merge_proj_kernel(h_ref, o0_ref, o1_ref, o2_ref, l0_ref, l1_ref, l2_ref, expand_ref, w_ref, g_ref, out_ref):
    l0, l1, l2 = l0_ref[...], l1_ref[...], l2_ref[...]
    m = jnp.maximum(jnp.maximum(l0, l1), l2)
    e0, e1, e2 = jnp.exp(l0 - m), jnp.exp(l1 - m), jnp.exp(l2 - m)
    inv = 1.0 / (e0 + e1 + e2)
    merged = None
    for e, o_ref in ((e0, o0_ref), (e1, o1_ref), (e2, o2_ref)):
        wgt = e * inv
        w_hi = wgt.astype(BF16)
        w_lo = (wgt - w_hi.astype(F32)).astype(BF16)
        term = (_mm(w_hi, expand_ref[...]) + _mm(w_lo, expand_ref[...])) * o_ref[...]
        merged = term if merged is None else merged + term
    y = _mm(merged.astype(BF16), w_ref[...])
    new_h = _gather_residues(h_ref, 0, h_ref.shape[1]) + _rms(y, g_ref[...])
    for r in range(out_ref.shape[1]):
        out_ref[:, r, :] = new_h[r * ATTN_BLOCK:(r + 1) * ATTN_BLOCK, :]


def _merge_proj(h, outs, lses, w, which, g):
    t, d = h.shape
    k = w.shape[-2]
    rows = RESIDUES_PER_STEP * ATTN_BLOCK
    per_span = MAX_DILATION // RESIDUES_PER_STEP
    row = lambda n: pl.BlockSpec((rows, n), lambda i: (i, 0))
    tokens = pl.BlockSpec((None, ATTN_BLOCK, RESIDUES_PER_STEP, d), lambda i: (i // per_span, 0, i % per_span, 0))
    head_of_col = jnp.arange(k) // ATTN_HEAD_DIM
    expand = (jnp.arange(LANES)[:, None] == head_of_col[None, :]).astype(BF16)
    view = (t // SPAN, ATTN_BLOCK, MAX_DILATION, d)
    return pl.pallas_call(
        _merge_proj_kernel,
        out_shape=jax.ShapeDtypeStruct(view, F32),
        grid=(t // rows,),
        in_specs=[tokens] + [row(k)] * 3 + [row(LANES)] * 3
        + [_resident((LANES, k)), _stacked(w.shape, which), _resident((1, d))],
        out_specs=tokens,
        compiler_params=_params("parallel"),
        name="merge_proj",
    )(h.reshape(view), *outs, *lses, expand, w, g.reshape(1, d)).reshape(t, d)


def _conv_ssd_mixer(h, bsz, seq, g_pre, g_post, w_in, j, conv_dw_w, conv_dw_b, conv_ln_g, conv_ln_b,
                    ssm_conv_w, ssm_conv_b, ssm_dt_bias, ssm_a_log, ssm_d, ssm_norm_g, w_out):
    conv_ch = conv_dw_w.shape[1]
    d_inner = SSM_HEADS * SSM_HEAD_DIM
    xbc_w = ssm_conv_w.shape[1]
    xbc, glu, zs, dt = _hyb_in(h, g_pre, w_in, (j,), conv_ch, d_inner, ssm_conv_w, ssm_conv_b, ssm_dt_bias, seq)
    ya = _convmod(glu.reshape(bsz, seq, conv_ch), conv_dw_w, conv_dw_b, conv_ln_g, conv_ln_b)
    yb = _ssd(xbc.reshape(bsz, seq, xbc_w), zs.reshape(bsz, seq, d_inner), dt.reshape(bsz, seq, LANES),
              ssm_a_log, ssm_d, ssm_norm_g)
    t = bsz * seq
    return _proj_res(h, ya.reshape(t, conv_ch), yb.reshape(t, d_inner), w_out, (j,), g_post)


def _attention_mixer(h, bsz, seq, g_pre, g_post, w_qkv, w_o, j):
    qkv = _qkv_proj(h, g_pre, w_qkv, (j,))
    outs, lses = [], []
    for g in range(len(DILATED_PATTERNS)):
        o, lse = _attn_group(qkv, g, bsz, seq)
        outs.append(o)
        lses.append(lse)
    return _merge_proj(h, outs, lses, w_o, (j,), g_post)


def kernel(x, norm_g, ffn_w1, ffn_w2, hyb_w_in, conv_dw_w, conv_dw_b, conv_ln_g, conv_ln_b, ssm_conv_w, ssm_conv_b, ssm_dt_bias, ssm_a_log, ssm_d, ssm_norm_g, hyb_w_out, attn_w_qkv, attn_w_o):
    bsz, seq, d = x.shape
    h = x.reshape(bsz * seq, d)
    w1, w2 = ffn_w1.astype(BF16), ffn_w2.astype(BF16)
    w_in = jnp.pad(hyb_w_in, ((0, 0), (0, 0), (0, LANES - SSM_HEADS))).astype(BF16)
    w_out, w_qkv, w_o = hyb_w_out.astype(BF16), attn_w_qkv.astype(BF16), attn_w_o.astype(BF16)
    for i in range(norm_g.shape[0]):
        g = norm_g[i]
        j = i // 2
        h = _ffn(h, g[0], g[1], w1, w2, (i, 0))
        if i % 2 == 0:
            h = _conv_ssd_mixer(h, bsz, seq, g[2], g[3], w_in, j, conv_dw_w[j], conv_dw_b[j],
                                conv_ln_g[j], conv_ln_b[j], ssm_conv_w[j], ssm_conv_b[j], ssm_dt_bias[j],
                                ssm_a_log[j], ssm_d[j], ssm_norm_g[j], w_out)
        else:
            h = _attention_mixer(h, bsz, seq, g[2], g[3], w_qkv, w_o, j)
        h = _ffn(h, g[4], g[5], w1, w2, (i, 1))
    return h.reshape(bsz, seq, d)
```

```python
import functools

import jax
import jax.numpy as jnp
from jax import lax
from jax.experimental import pallas as pl
from jax.experimental.pallas import tpu as pltpu

NORM_EPS = 1e-6
CONV_WIDTH = 31
SSM_HEAD_DIM = 64
SSM_HEADS = 16
SSM_GROUPS = 2
SSM_STATE = 128
SSM_CONV_WIDTH = 4
SSM_CHUNK = 128
ATTN_HEAD_DIM = 64
ATTN_HEADS = 8
DILATED_PATTERNS = ((128, 1), (512, 4), (2048, 16))
ATTN_BLOCK = 128

LANES = 128
SUBLANES = 8
VMEM_LIMIT_BYTES = 56 * 1024 * 1024
MASK_VALUE = -1e30

F32 = jnp.float32
BF16 = jnp.bfloat16


def _params(*sem):
    return pltpu.CompilerParams(dimension_semantics=sem, vmem_limit_bytes=VMEM_LIMIT_BYTES)


def _resident(shape):
    nd = len(shape)
    return pl.BlockSpec(shape, lambda *_: (0,) * nd, pipeline_mode=pl.Buffered(1))


def _stacked(shape, which):
    lead = len(which)
    nd = len(shape)
    return pl.BlockSpec((None,) * lead + tuple(shape[lead:]), lambda *_: tuple(which) + (0,) * (nd - lead),
                        pipeline_mode=pl.Buffered(1))


def _rms(x, g):
    return x * lax.rsqrt(jnp.mean(x * x, axis=-1, keepdims=True) + NORM_EPS) * g


def _silu(x):
    return x * jax.nn.sigmoid(x)


def _mm(a, b):
    return jnp.dot(a, b, preferred_element_type=F32)


FFN_ROW_GROUPS = 4


def _ffn_kernel(h_ref, gpre_ref, gpost_ref, w1_ref, w2_ref, o_ref):
    f = w2_ref.shape[0]
    half = h_ref.shape[0] // FFN_ROW_GROUPS
    for part in range(FFN_ROW_GROUPS):
        rows = slice(part * half, (part + 1) * half)
        h = h_ref[rows, :]
        u = _rms(h, gpre_ref[...]).astype(BF16)
        gate = _mm(u, w1_ref[:, :f])
        up = _mm(u, w1_ref[:, f:])
        act = (_silu(gate) * up).astype(BF16)
        y = _mm(act, w2_ref[...])
        o_ref[rows, :] = h + 0.5 * _rms(y, gpost_ref[...])


def _ffn(h, g_pre, g_post, w1, w2, which, tm=512):
    t, d = h.shape
    f = w2.shape[-2]
    row = pl.BlockSpec((tm, d), lambda i: (i, 0))
    return pl.pallas_call(
        _ffn_kernel,
        out_shape=jax.ShapeDtypeStruct((t, d), F32),
        grid=(t // tm,),
        in_specs=[row, _resident((1, d)), _resident((1, d)),
                  _stacked(w1.shape, which), _stacked(w2.shape, which)],
        out_specs=row,
        compiler_params=_params("parallel"),
        name="ffn",
    )(h, g_pre.reshape(1, d), g_post.reshape(1, d), w1, w2)


RESIDUES_PER_STEP = SUBLANES


RESIDUES_PER_GROUP = 4


def _gather_residues(h_ref, r0, count):
    return jnp.concatenate([h_ref[:, r, :] for r in range(r0, r0 + count)], axis=0)


def _qkv_kernel(h_ref, g_ref, w_ref, o_ref):
    group_rows = RESIDUES_PER_GROUP * ATTN_BLOCK
    for part in range(RESIDUES_PER_STEP // RESIDUES_PER_GROUP):
        u = _rms(_gather_residues(h_ref, part * RESIDUES_PER_GROUP, RESIDUES_PER_GROUP), g_ref[...]).astype(BF16)
        o_ref[part * group_rows:(part + 1) * group_rows, :] = _mm(u, w_ref[...]).astype(o_ref.dtype)


def _qkv_proj(h, g, w, which):
    t, d = h.shape
    n = w.shape[-1]
    rows = RESIDUES_PER_STEP * ATTN_BLOCK
    per_span = MAX_DILATION // RESIDUES_PER_STEP
    view = h.reshape(t // SPAN, ATTN_BLOCK, MAX_DILATION, d)
    return pl.pallas_call(
        _qkv_kernel,
        out_shape=jax.ShapeDtypeStruct((t, n), BF16),
        grid=(t // rows,),
        in_specs=[pl.BlockSpec((None, ATTN_BLOCK, RESIDUES_PER_STEP, d),
                               lambda i: (i // per_span, 0, i % per_span, 0)),
                  _resident((1, d)), _stacked(w.shape, which)],
        out_specs=pl.BlockSpec((rows, n), lambda i: (i, 0)),
        compiler_params=_params("parallel"),
        name="qkv_proj",
    )(view, g.reshape(1, d), w)


HYB_IN_ROW_GROUPS = 4


def _hyb_in_kernel(h_ref, g_ref, w_ref, cw_ref, cb_ref, dtb_ref,
                   xbc_ref, glu_ref, zs_ref, dt_ref, xbuf_ref, *, tiles_per_seq):
    tm = h_ref.shape[0]
    s0 = 2 * glu_ref.shape[1]
    s1 = s0 + zs_ref.shape[1]
    s2 = s1 + xbc_ref.shape[1]
    wg_ref, wz_ref, wx_ref, wd_ref = w_ref.at[:, :s0], w_ref.at[:, s0:s1], w_ref.at[:, s1:s2], w_ref.at[:, s2:]
    i = pl.program_id(0)

    @pl.when(i % tiles_per_seq == 0)
    def _():
        xbuf_ref[0:SUBLANES, :] = jnp.zeros((SUBLANES, xbuf_ref.shape[1]), F32)

    @pl.when(i % tiles_per_seq != 0)
    def _():
        xbuf_ref[0:SUBLANES, :] = xbuf_ref[tm:tm + SUBLANES, :]

    rows_per_group = tm // HYB_IN_ROW_GROUPS
    first = SUBLANES - (SSM_CONV_WIDTH - 1)
    ch = glu_ref.shape[1]
    for part in range(HYB_IN_ROW_GROUPS):
        r0 = part * rows_per_group
        rows = slice(r0, r0 + rows_per_group)
        u = _rms(h_ref[rows, :], g_ref[...]).astype(BF16)
        xbuf_ref[SUBLANES + r0:SUBLANES + r0 + rows_per_group, :] = _mm(u, wx_ref[...])
        acc = jnp.broadcast_to(cb_ref[...], (rows_per_group, xbuf_ref.shape[1]))
        for k in range(SSM_CONV_WIDTH):
            acc = acc + xbuf_ref[r0 + first + k:r0 + first + k + rows_per_group, :] * cw_ref[k:k + 1, :]
        xbc_ref[rows, :] = _silu(acc)
        vg = _mm(u, wg_ref[...])
        glu_ref[rows, :] = vg[:, :ch] * jax.nn.sigmoid(vg[:, ch:])
        zs_ref[rows, :] = _silu(_mm(u, wz_ref[...]))
        dt_ref[rows, :] = jax.nn.softplus(_mm(u, wd_ref[...]) + dtb_ref[...])


def _hyb_in(h, g, w_in, which, conv_ch, d_inner, ssm_conv_w, ssm_conv_b, dt_bias, seq, tm=512):
    t, d = h.shape
    xbc_w = ssm_conv_w.shape[1]
    assert w_in.shape[-1] == 2 * conv_ch + d_inner + xbc_w + LANES
    cw = jnp.zeros((SUBLANES, xbc_w), F32).at[:SSM_CONV_WIDTH].set(ssm_conv_w)
    row = lambda n: pl.BlockSpec((tm, n), lambda i: (i, 0))
    widths = (xbc_w, conv_ch, d_inner, LANES)
    return pl.pallas_call(
        functools.partial(_hyb_in_kernel, tiles_per_seq=seq // tm),
        out_shape=[jax.ShapeDtypeStruct((t, n), F32) for n in widths],
        grid=(t // tm,),
        in_specs=[row(d), _resident((1, d)), _stacked(w_in.shape, which),
                  _resident((SUBLANES, xbc_w)), _resident((1, xbc_w)), _resident((1, LANES))],
        out_specs=[row(n) for n in widths],
        scratch_shapes=[pltpu.VMEM((tm + SUBLANES, xbc_w), F32)],
        compiler_params=_params("arbitrary"),
        name="hyb_in",
    )(h, g.reshape(1, d), w_in, cw, ssm_conv_b.reshape(1, xbc_w),
      jnp.pad(dt_bias, (0, LANES - SSM_HEADS)).reshape(1, LANES))


CONV_TAIL = 32
CONV_ROW_SPLIT = 4


def _convmod_kernel(glu_ref, w_ref, b_ref, lg_ref, lb_ref, o_ref, buf_ref, hid_ref, *, tl):
    ch = glu_ref.shape[-1]
    nt = ch // LANES

    @pl.when(pl.program_id(1) == 0)
    def _():
        buf_ref[0, :, 0:CONV_TAIL, :] = jnp.zeros((nt, CONV_TAIL, LANES), F32)

    @pl.when(pl.program_id(1) > 0)
    def _():
        buf_ref[0, :, 0:CONV_TAIL, :] = buf_ref[0, :, tl:tl + CONV_TAIL, :]

    for c in range(nt):
        buf_ref[0, c, CONV_TAIL:CONV_TAIL + tl, :] = glu_ref[:, c * LANES:(c + 1) * LANES]

    first = CONV_TAIL - (CONV_WIDTH - 1)
    shifted_rows = tl + CONV_TAIL - SUBLANES
    rows = tl // CONV_ROW_SPLIT
    tiles = (rows // SUBLANES, SUBLANES, LANES)

    def lane_tile(c, carry):
        for r in range(1, SUBLANES):
            buf_ref[r, c, 0:shifted_rows, :] = buf_ref[0, c, r:r + shifted_rows, :]
        wc = w_ref[c]
        wk = [jnp.broadcast_to(wc[k:k + 1, :], (SUBLANES, LANES))[None] for k in range(CONV_WIDTH)]
        for part in range(CONV_ROW_SPLIT):
            acc = jnp.broadcast_to(b_ref[c], tiles)
            for k in range(CONV_WIDTH):
                shift = (first + k) % SUBLANES
                row = part * rows + first + k - shift
                acc = acc + buf_ref[shift, c, row:row + rows, :].reshape(tiles) * wk[k]
            hid_ref[c, part * rows:(part + 1) * rows, :] = acc.reshape(rows, LANES)
        return carry

    lax.fori_loop(0, nt, lane_tile, 0)

    hid = jnp.concatenate([hid_ref[c] for c in range(nt)], axis=1)
    xc = hid - jnp.mean(hid, axis=-1, keepdims=True)
    y = xc * lax.rsqrt(jnp.mean(xc * xc, axis=-1, keepdims=True) + NORM_EPS)
    o_ref[...] = _silu(y * lg_ref[...] + lb_ref[...]).astype(o_ref.dtype)


def _convmod(glu, w, b, ln_g, ln_b, tl=512):
    bsz, seq, ch = glu.shape
    nt = ch // LANES
    wpad = jnp.zeros((CONV_TAIL, ch), F32).at[:CONV_WIDTH].set(w)
    wpad = wpad.reshape(CONV_TAIL, nt, LANES).transpose(1, 0, 2)
    vec = _resident((1, ch))
    return pl.pallas_call(
        functools.partial(_convmod_kernel, tl=tl),
        out_shape=jax.ShapeDtypeStruct((bsz, seq, ch), BF16),
        grid=(bsz, seq // tl),
        in_specs=[pl.BlockSpec((None, tl, ch), lambda bi, i: (bi, i, 0)),
                  _resident((nt, CONV_TAIL, LANES)), _resident((nt, 1, LANES)), vec, vec],
        out_specs=pl.BlockSpec((None, tl, ch), lambda bi, i: (bi, i, 0)),
        scratch_shapes=[pltpu.VMEM((SUBLANES, nt, tl + CONV_TAIL, LANES), F32), pltpu.VMEM((nt, tl, LANES), F32)],
        compiler_params=_params("parallel", "arbitrary"),
        name="convmod",
    )(glu, wpad, b.reshape(nt, 1, LANES), ln_g.reshape(1, ch), ln_b.reshape(1, ch))


SSD_CHUNKS_PER_STEP = 8


def _ssd_kernel(xbc_ref, zs_ref, dt_ref, alog_ref, dskip_ref, ng_ref, expand_ref, tri_ref, o_ref, state_ref):
    q = SSM_CHUNK
    nchunk = xbc_ref.shape[0] // q
    d_inner = SSM_HEADS * SSM_HEAD_DIM
    gcols = d_inner // SSM_GROUPS
    nst = SSM_STATE
    chunk = lambda c: slice(c * q, (c + 1) * q)

    @pl.when(pl.program_id(1) == 0)
    def _():
        state_ref[...] = jnp.zeros(state_ref.shape, F32)

    dt = dt_ref[...]
    a = dt * (-jnp.exp(alog_ref[...]))
    tri = tri_ref[...]
    a_hi = a.astype(BF16)
    a_mid = (a - a_hi.astype(F32)).astype(BF16)
    a_lo = (a - a_hi.astype(F32) - a_mid.astype(F32)).astype(BF16)
    acums = [_mm(tri, a_hi[chunk(c)]) + _mm(tri, a_mid[chunk(c)]) + _mm(tri, a_lo[chunk(c)])
             for c in range(nchunk)]
    acum = jnp.concatenate(acums, axis=0)
    dstate = jnp.concatenate([jnp.exp(ac[q - 1:q, :] - ac) for ac in acums], axis=0)
    per_head = jnp.concatenate([jnp.exp(acum), dstate * dt], axis=0)
    ph_hi = per_head.astype(BF16)
    ph_lo = (per_head - ph_hi.astype(F32)).astype(BF16)
    full = _mm(ph_hi, expand_ref[...]) + _mm(ph_lo, expand_ref[...])
    rows = nchunk * q
    eacum_full = full[0:rows]
    xs = xbc_ref[:, :d_inner]
    xw = (xs * full[rows:2 * rows]).astype(BF16)
    causal = lax.broadcasted_iota(jnp.int32, (q, q), 0) >= lax.broadcasted_iota(jnp.int32, (q, q), 1)
    low_half = lax.broadcasted_iota(jnp.int32, (q, LANES), 1) < SSM_HEAD_DIM
    heads_per_group = SSM_HEADS // SSM_GROUPS

    state = state_ref[...]
    for c in range(nchunk):
        r = chunk(c)
        ac = acums[c]
        ac_t = ac.T
        dt_t = dt[r].T
        y_diag, y_off, new_states = [], [], []
        for g in range(SSM_GROUPS):
            bg = xbc_ref[r, d_inner + g * nst:d_inner + (g + 1) * nst].astype(BF16)
            cg = xbc_ref[r, d_inner + (SSM_GROUPS + g) * nst:d_inner + (SSM_GROUPS + g + 1) * nst].astype(BF16)
            gsl = slice(g * gcols, (g + 1) * gcols)
            cb = lax.dot_general(cg, bg, (((1,), (1,)), ((), ())), preferred_element_type=F32)
            y_off.append(_mm(cg, state[:, gsl].astype(BF16)))
            new_states.append(lax.dot_general(bg, xw[r, gsl], (((0,), (0,)), ((), ())),
                                              preferred_element_type=F32))
            for pair in range(heads_per_group // 2):
                h0 = g * heads_per_group + 2 * pair
                masks = []
                for h in (h0, h0 + 1):
                    diff = ac[:, h:h + 1] - ac_t[h:h + 1, :]
                    decay_in = jnp.exp(jnp.where(causal, diff, MASK_VALUE)) * dt_t[h:h + 1, :]
                    masks.append((cb * decay_in).astype(BF16))
                lhs = jnp.concatenate(masks, axis=1)
                xp = xs[r, h0 * SSM_HEAD_DIM:(h0 + 2) * SSM_HEAD_DIM]
                rhs = jnp.concatenate([jnp.where(low_half, xp, 0.0), jnp.where(low_half, 0.0, xp)],
                                      axis=0).astype(BF16)
                y_diag.append(_mm(lhs, rhs))
        decay = eacum_full[r]
        y = (jnp.concatenate(y_diag, axis=1) + jnp.concatenate(y_off, axis=1) * decay
             + dskip_ref[...] * xs[r])
        state = state * decay[q - 1:q, :] + jnp.concatenate(new_states, axis=1)
        y = y * zs_ref[r, :]
        o_ref[r, :] = _rms(y, ng_ref[...]).astype(o_ref.dtype)
    state_ref[...] = state


def _ssd(xbc, zs, dt, a_log, d_skip, norm_g):
    bsz, seq, xbc_w = xbc.shape
    q = SSM_CHUNK
    rows = SSD_CHUNKS_PER_STEP * q
    d_inner = SSM_HEADS * SSM_HEAD_DIM
    head_of_col = jnp.arange(d_inner) // SSM_HEAD_DIM
    expand = (jnp.arange(LANES)[:, None] == head_of_col[None, :]).astype(BF16)
    tri = (jnp.arange(q)[:, None] >= jnp.arange(q)[None, :]).astype(BF16)
    return pl.pallas_call(
        _ssd_kernel,
        out_shape=jax.ShapeDtypeStruct((bsz, seq, d_inner), BF16),
        grid=(bsz, seq // rows),
        in_specs=[pl.BlockSpec((None, rows, xbc_w), lambda bi, i: (bi, i, 0)),
                  pl.BlockSpec((None, rows, d_inner), lambda bi, i: (bi, i, 0)),
                  pl.BlockSpec((None, rows, LANES), lambda bi, i: (bi, i, 0)),
                  _resident((1, LANES)), _resident((1, d_inner)), _resident((1, d_inner)),
                  _resident((LANES, d_inner)), _resident((q, q))],
        out_specs=pl.BlockSpec((None, rows, d_inner), lambda bi, i: (bi, i, 0)),
        scratch_shapes=[pltpu.VMEM((SSM_STATE, d_inner), F32)],
        compiler_params=_params("parallel", "arbitrary"),
        name="ssd",
    )(xbc, zs, dt, jnp.pad(a_log, (0, LANES - SSM_HEADS)).reshape(1, LANES),
      jnp.repeat(d_skip, SSM_HEAD_DIM).reshape(1, d_inner), norm_g.reshape(1, d_inner), expand, tri)


def _proj_res_kernel(h_ref, xa_ref, xb_ref, w_ref, g_ref, o_ref):
    ka = xa_ref.shape[1]
    y = _mm(xa_ref[...], w_ref[:ka, :]) + _mm(xb_ref[...], w_ref[ka:, :])
    o_ref[...] = h_ref[...] + _rms(y, g_ref[...])


def _proj_res(h, xa, xb, w, which, g, tm=512):
    t, d = h.shape
    ka, kb = xa.shape[1], xb.shape[1]
    assert w.shape[-2] == ka + kb
    row = lambda n: pl.BlockSpec((tm, n), lambda i: (i, 0))
    return pl.pallas_call(
        _proj_res_kernel,
        out_shape=jax.ShapeDtypeStruct((t, d), F32),
        grid=(t // tm,),
        in_specs=[row(d), row(ka), row(kb), _stacked(w.shape, which), _resident((1, d))],
        out_specs=row(d),
        compiler_params=_params("parallel"),
        name="proj_res",
    )(h, xa, xb, w, g.reshape(1, d))


MAX_DILATION = 16
SPAN = ATTN_BLOCK * MAX_DILATION
ATTN_WIDTH = ATTN_HEADS * ATTN_HEAD_DIM


def _attend(jobs, bias_ref, scores_ref):
    dh = ATTN_HEAD_DIM
    nq = ATTN_BLOCK
    npairs = ATTN_HEADS // 2
    low_head = lax.broadcasted_iota(jnp.int32, (nq, LANES), 1) < dh
    for j, (q, k, _, _) in enumerate(jobs):
        for pair in range(npairs):
            cols = slice(pair * LANES, (pair + 1) * LANES)
            qp = q[:, cols]
            q2 = jnp.concatenate([jnp.where(low_head, qp, 0), jnp.where(low_head, 0, qp)], axis=0)
            scores_ref[j * npairs + pair] = lax.dot_general(k[:, cols], q2, (((1,), (1,)), ((), ())),
                                                            preferred_element_type=F32)
    results = []
    for j, (_, _, v, table) in enumerate(jobs):
        outs, lses = [], []
        for pair in range(npairs):
            cols = slice(pair * LANES, (pair + 1) * LANES)
            st = scores_ref[j * npairs + pair] + bias_ref[table, pair]
            m = jnp.max(st, axis=0, keepdims=True)
            p = jnp.exp(st - m)
            l = jnp.sum(p, axis=0, keepdims=True)
            ot = lax.dot_general(v[:, cols], p.astype(BF16), (((0,), (0,)), ((), ())), preferred_element_type=F32)
            ot = ot / l
            lse = m + jnp.log(l)
            for hh in range(2):
                outs.append(ot[hh * dh:(hh + 1) * dh, hh * nq:(hh + 1) * nq])
                lses.append(lse[:, hh * nq:(hh + 1) * nq])
        lse_t = jnp.concatenate(lses + [jnp.zeros((LANES - ATTN_HEADS, nq), F32)], axis=0)
        results.append((jnp.concatenate(outs, axis=0).T, lse_t.T))
    return results


BLOCKS_PER_STEP = 4


def _reset_carry(kprev_ref, vprev_ref):
    @pl.when(pl.program_id(2) == 0)
    def _():
        kprev_ref[...] = jnp.zeros(kprev_ref.shape, kprev_ref.dtype)
        vprev_ref[...] = jnp.zeros(vprev_ref.shape, vprev_ref.dtype)


def _attn_pairs_kernel(q_ref, kprev_ref, k_ref, vprev_ref, v_ref, bias_ref, o_ref, lse_ref, scores_ref):
    table = jnp.minimum(pl.program_id(2), 1)
    scale = ATTN_HEAD_DIM ** -0.5
    blocks = [slice(b * ATTN_BLOCK, (b + 1) * ATTN_BLOCK) for b in range(BLOCKS_PER_STEP)]
    jobs = [(q_ref[rows, :] * scale,
             jnp.concatenate([kprev_ref[rows, :], k_ref[rows, :]], axis=0),
             jnp.concatenate([vprev_ref[rows, :], v_ref[rows, :]], axis=0), table) for rows in blocks]
    for rows, (o, lse) in zip(blocks, _attend(jobs, bias_ref, scores_ref)):
        o_ref[rows, :] = o.astype(o_ref.dtype)
        lse_ref[rows, :] = lse


def _attn_chain_kernel(q_ref, k_ref, v_ref, bias_ref, o_ref, lse_ref, scores_ref, kprev_ref, vprev_ref, *, runs,
                       run_rows):
    _reset_carry(kprev_ref, vprev_ref)
    part = run_rows // BLOCKS_PER_STEP
    scale = ATTN_HEAD_DIM ** -0.5

    def parts(ref):
        x = ref[...].astype(F32)
        return [x[:, p * part:(p + 1) * part, :].reshape(ATTN_BLOCK, x.shape[-1])
                for p in range(BLOCKS_PER_STEP)]

    k32, v32 = parts(k_ref), parts(v_ref)
    qs = [x.astype(BF16) for x in parts(q_ref)]
    ks = [x.astype(BF16) for x in k32]
    vs = [x.astype(BF16) for x in v32]
    kprev = [kprev_ref[...].astype(BF16)] + ks[:-1]
    vprev = [vprev_ref[...].astype(BF16)] + vs[:-1]
    first_table = jnp.minimum(pl.program_id(2), 1)
    jobs = [(qs[p] * scale, jnp.concatenate([kprev[p], ks[p]], axis=0),
             jnp.concatenate([vprev[p], vs[p]], axis=0), first_table if p == 0 else 1)
            for p in range(BLOCKS_PER_STEP)]
    results = _attend(jobs, bias_ref, scores_ref)

    def interleave(xs):
        return jnp.concatenate([x.reshape(runs, part, x.shape[-1]) for x in xs], axis=1)

    o_ref[...] = interleave([o for o, _ in results]).astype(o_ref.dtype)
    lse_ref[...] = interleave([lse for _, lse in results])
    kprev_ref[...] = k32[-1]
    vprev_ref[...] = v32[-1]


def _attn_bias(window, dilation, offsets):
    steps = window // dilation
    slopes = jnp.exp2(-8.0 * jnp.arange(1, ATTN_HEADS + 1, dtype=F32) / ATTN_HEADS)
    kj = jnp.concatenate([offsets - ATTN_BLOCK, offsets])[:, None]
    qi = offsets[None, :]
    rel = qi - kj
    valid = (rel >= 0) & (rel <= steps)
    bias = -slopes[:, None, None] * (rel * dilation).astype(F32)
    later = jnp.where(valid[None], bias, MASK_VALUE)
    first = jnp.where((valid & (kj >= 0))[None], bias, MASK_VALUE)
    table = jnp.stack([first, later], axis=0)
    table = table.reshape(2, ATTN_HEADS // 2, 2, 2 * ATTN_BLOCK, ATTN_BLOCK)
    return table.transpose(0, 1, 3, 2, 4).reshape(2, ATTN_HEADS // 2, 2 * ATTN_BLOCK, 2 * ATTN_BLOCK)


def _attn_group(qkv, g, bsz, seq):
    window, dilation = DILATED_PATTERNS[g]
    ngroups = len(DILATED_PATTERNS)
    assert window // dilation <= ATTN_BLOCK and seq % SPAN == 0 and MAX_DILATION % dilation == 0
    t = bsz * seq
    nspan = seq // SPAN
    ncols = qkv.shape[1]
    col = lambda which: which * ngroups + g
    bias_spec = pl.BlockSpec((2, ATTN_HEADS // 2, 2 * ATTN_BLOCK, 2 * ATTN_BLOCK), lambda *_: (0, 0, 0, 0),
                             pipeline_mode=pl.Buffered(1))
    u = jnp.arange(ATTN_BLOCK)
    nb = BLOCKS_PER_STEP
    scores_scratch = pltpu.VMEM((nb * ATTN_HEADS // 2, 2 * ATTN_BLOCK, 2 * ATTN_BLOCK), F32)

    if dilation == MAX_DILATION:
        groups = MAX_DILATION // nb
        rows = nb * ATTN_BLOCK
        cur = lambda which: pl.BlockSpec((rows, ATTN_WIDTH),
                                         lambda b, r, j: ((b * nspan + j) * groups + r, col(which)))
        prev = lambda which: pl.BlockSpec(
            (rows, ATTN_WIDTH), lambda b, r, j: ((b * nspan + jnp.maximum(j - 1, 0)) * groups + r, col(which)))
        out = lambda w: pl.BlockSpec((rows, w), lambda b, r, j: ((b * nspan + j) * groups + r, 0))
        o, lse = pl.pallas_call(
            _attn_pairs_kernel,
            out_shape=[jax.ShapeDtypeStruct((t, ATTN_WIDTH), BF16), jax.ShapeDtypeStruct((t, LANES), F32)],
            grid=(bsz, groups, nspan),
            in_specs=[cur(0), prev(1), cur(1), prev(2), cur(2), bias_spec],
            out_specs=[out(ATTN_WIDTH), out(LANES)],
            scratch_shapes=[scores_scratch],
            compiler_params=_params("parallel", "parallel", "arbitrary"),
            name=f"attn_d{dilation}",
        )(qkv, qkv, qkv, qkv, qkv, _attn_bias(window, dilation, u))
        return o, lse

    runs = MAX_DILATION // dilation
    run_rows = nb * ATTN_BLOCK // runs
    steps_per_span = ATTN_BLOCK // run_rows
    shape = (bsz * nspan, runs, dilation, steps_per_span, run_rows)
    blk = lambda w: (None, runs, None, None, run_rows, w)
    at = lambda b, r, s, c: (b * nspan + s // steps_per_span, 0, r, s % steps_per_span, 0, c)
    cur = lambda which: pl.BlockSpec(blk(ATTN_WIDTH), lambda b, r, s: at(b, r, s, col(which)))
    out = lambda w: pl.BlockSpec(blk(w), lambda b, r, s: at(b, r, s, 0))
    view = qkv.reshape(shape + (ncols,))
    part = run_rows // nb
    offsets = runs * (u % part) + u // part
    carry = pltpu.VMEM((ATTN_BLOCK, ATTN_WIDTH), F32)
    o, lse = pl.pallas_call(
        functools.partial(_attn_chain_kernel, runs=runs, run_rows=run_rows),
        out_shape=[jax.ShapeDtypeStruct(shape + (ATTN_WIDTH,), BF16), jax.ShapeDtypeStruct(shape + (LANES,), F32)],
        grid=(bsz, dilation, nspan * steps_per_span),
        in_specs=[cur(0), cur(1), cur(2), bias_spec],
        out_specs=[out(ATTN_WIDTH), out(LANES)],
        scratch_shapes=[scores_scratch, carry, carry],
        compiler_params=_params("parallel", "parallel", "arbitrary"),
        name=f"attn_d{dilation}",
    )(view, view, view, _attn_bias(window, dilation, offsets))
    return o.reshape(t, ATTN_WIDTH), lse.reshape(t, LANES)


def _merge_proj_kernel(h_ref, o0_ref, o1_ref, o2_ref, l0_ref, l1_ref, l2_ref, expand_ref, w_ref, g_ref, out_ref):
    l0, l1, l2 = l0_ref[...], l1_ref[...], l2_ref[...]
    m = jnp.maximum(jnp.maximum(l0, l1), l2)
    e0, e1, e2 = jnp.exp(l0 - m), jnp.exp(l1 - m), jnp.exp(l2 - m)
    inv = 1.0 / (e0 + e1 + e2)
    merged = None
    for e, o_ref in ((e0, o0_ref), (e1, o1_ref), (e2, o2_ref)):
        wgt = e * inv
        w_hi = wgt.astype(BF16)
        w_lo = (wgt - w_hi.astype(F32)).astype(BF16)
        term = (_mm(w_hi, expand_ref[...]) + _mm(w_lo, expand_ref[...])) * o_ref[...]
        merged = term if merged is None else merged + term
    y = _rms(_mm(merged.astype(BF16), w_ref[...]), g_ref[...])
    for r in range(out_ref.shape[1]):
        out_ref[:, r, :] = y[r * ATTN_BLOCK:(r + 1) * ATTN_BLOCK, :]
    out_ref[...] = out_ref[...] + h_ref[...]


def _merge_proj(h, outs, lses, w, which, g):
    t, d = h.shape
    k = w.shape[-2]
    rows = RESIDUES_PER_STEP * ATTN_BLOCK
    per_span = MAX_DILATION // RESIDUES_PER_STEP
    row = lambda n: pl.BlockSpec((rows, n), lambda i: (i, 0))
    tokens = pl.BlockSpec((None, ATTN_BLOCK, RESIDUES_PER_STEP, d), lambda i: (i // per_span, 0, i % per_span, 0))
    head_of_col = jnp.arange(k) // ATTN_HEAD_DIM
    expand = (jnp.arange(LANES)[:, None] == head_of_col[None, :]).astype(BF16)
    view = (t // SPAN, ATTN_BLOCK, MAX_DILATION, d)
    return pl.pallas_call(
        _merge_proj_kernel,
        out_shape=jax.ShapeDtypeStruct(view, F32),
        grid=(t // rows,),
        in_specs=[tokens] + [row(k)] * 3 + [row(LANES)] * 3
        + [_resident((LANES, k)), _stacked(w.shape, which), _resident((1, d))],
        out_specs=tokens,
        compiler_params=_params("parallel"),
        name="merge_proj",
    )(h.reshape(view), *outs, *lses, expand, w, g.reshape(1, d)).reshape(t, d)


def _conv_ssd_mixer(h, bsz, seq, g_pre, g_post, w_in, j, conv_dw_w, conv_dw_b, conv_ln_g, conv_ln_b,
                    ssm_conv_w, ssm_conv_b, ssm_dt_bias, ssm_a_log, ssm_d, ssm_norm_g, w_out):
    conv_ch = conv_dw_w.shape[1]
    d_inner = SSM_HEADS * SSM_HEAD_DIM
    xbc_w = ssm_conv_w.shape[1]
    xbc, glu, zs, dt = _hyb_in(h, g_pre, w_in, (j,), conv_ch, d_inner, ssm_conv_w, ssm_conv_b, ssm_dt_bias, seq)
    ya = _convmod(glu.reshape(bsz, seq, conv_ch), conv_dw_w, conv_dw_b, conv_ln_g, conv_ln_b)
    yb = _ssd(xbc.reshape(bsz, seq, xbc_w), zs.reshape(bsz, seq, d_inner), dt.reshape(bsz, seq, LANES),
              ssm_a_log, ssm_d, ssm_norm_g)
    t = bsz * seq
    return _proj_res(h, ya.reshape(t, conv_ch), yb.reshape(t, d_inner), w_out, (j,), g_post)


def _attention_mixer(h, bsz, seq, g_pre, g_post, w_qkv, w_o, j):
    qkv = _qkv_proj(h, g_pre, w_qkv, (j,))
    outs, lses = [], []
    for g in range(len(DILATED_PATTERNS)):
        o, lse = _attn_group(qkv, g, bsz, seq)
        outs.append(o)
        lses.append(lse)
    return _merge_proj(h, outs, lses, w_o, (j,), g_post)


def kernel(x, norm_g, ffn_w1, ffn_w2, hyb_w_in, conv_dw_w, conv_dw_b, conv_ln_g, conv_ln_b, ssm_conv_w, ssm_conv_b, ssm_dt_bias, ssm_a_log, ssm_d, ssm_norm_g, hyb_w_out, attn_w_qkv, attn_w_o):
    bsz, seq, d = x.shape
    h = x.reshape(bsz * seq, d)
    w1, w2 = ffn_w1.astype(BF16), ffn_w2.astype(BF16)
    w_in = jnp.pad(hyb_w_in, ((0, 0), (0, 0), (0, LANES - SSM_HEADS))).astype(BF16)
    w_out, w_qkv, w_o = hyb_w_out.astype(BF16), attn_w_qkv.astype(BF16), attn_w_o.astype(BF16)
    for i in range(norm_g.shape[0]):
        g = norm_g[i]
        j = i // 2
        h = _ffn(h, g[0], g[1], w1, w2, (i, 0))
        if i % 2 == 0:
            h = _conv_ssd_mixer(h, bsz, seq, g[2], g[3], w_in, j, conv_dw_w[j], conv_dw_b[j],
                                conv_ln_g[j], conv_ln_b[j], ssm_conv_w[j], ssm_conv_b[j], ssm_dt_bias[j],
                                ssm_a_log[j], ssm_d[j], ssm_norm_g[j], w_out)
        else:
            h = _attention_mixer(h, bsz, seq, g[2], g[3], w_qkv, w_o, j)
        h = _ffn(h, g[4], g[5], w1, w2, (i, 1))
    return h.reshape(bsz, seq, d)
```

```python
import functools

import jax
import jax.numpy as jnp
from jax import lax
from jax.experimental import pallas as pl
from jax.experimental.pallas import tpu as pltpu

NORM_EPS = 1e-6
CONV_WIDTH = 31
SSM_HEAD_DIM = 64
SSM_HEADS = 16
SSM_GROUPS = 2
SSM_STATE = 128
SSM_CONV_WIDTH = 4
SSM_CHUNK = 128
ATTN_HEAD_DIM = 64
ATTN_HEADS = 8
DILATED_PATTERNS = ((128, 1), (512, 4), (2048, 16))
ATTN_BLOCK = 128

LANES = 128
SUBLANES = 8
VMEM_LIMIT_BYTES = 56 * 1024 * 1024
MASK_VALUE = -1e30

F32 = jnp.float32
BF16 = jnp.bfloat16


def _params(*sem):
    return pltpu.CompilerParams(dimension_semantics=sem, vmem_limit_bytes=VMEM_LIMIT_BYTES)


def _resident(shape):
    nd = len(shape)
    return pl.BlockSpec(shape, lambda *_: (0,) * nd, pipeline_mode=pl.Buffered(1))


def _stacked(shape, which):
    lead = len(which)
    nd = len(shape)
    return pl.BlockSpec((None,) * lead + tuple(shape[lead:]), lambda *_: tuple(which) + (0,) * (nd - lead),
                        pipeline_mode=pl.Buffered(1))


def _rms(x, g):
    return x * lax.rsqrt(jnp.mean(x * x, axis=-1, keepdims=True) + NORM_EPS) * g


def _silu(x):
    return x * jax.nn.sigmoid(x)


def _mm(a, b):
    return jnp.dot(a, b, preferred_element_type=F32)


FFN_ROW_GROUPS = 4


def _ffn_kernel(h_ref, gpre_ref, gpost_ref, w1_ref, w2_ref, o_ref):
    f = w2_ref.shape[0]
    half = h_ref.shape[0] // FFN_ROW_GROUPS
    for part in range(FFN_ROW_GROUPS):
        rows = slice(part * half, (part + 1) * half)
        h = h_ref[rows, :]
        u = _rms(h, gpre_ref[...]).astype(BF16)
        gate = _mm(u, w1_ref[:, :f])
        up = _mm(u, w1_ref[:, f:])
        act = (_silu(gate) * up).astype(BF16)
        y = _mm(act, w2_ref[...])
        o_ref[rows, :] = h + 0.5 * _rms(y, gpost_ref[...])


def _ffn(h, g_pre, g_post, w1, w2, which, tm=512):
    t, d = h.shape
    f = w2.shape[-2]
    row = pl.BlockSpec((tm, d), lambda i: (i, 0))
    return pl.pallas_call(
        _ffn_kernel,
        out_shape=jax.ShapeDtypeStruct((t, d), F32),
        grid=(t // tm,),
        in_specs=[row, _resident((1, d)), _resident((1, d)),
                  _stacked(w1.shape, which), _stacked(w2.shape, which)],
        out_specs=row,
        compiler_params=_params("parallel"),
        name="ffn",
    )(h, g_pre.reshape(1, d), g_post.reshape(1, d), w1, w2)


RESIDUES_PER_STEP = SUBLANES


RESIDUES_PER_GROUP = 4


def _gather_residues(h_ref, r0, count):
    return jnp.concatenate([h_ref[:, r, :] for r in range(r0, r0 + count)], axis=0)


def _qkv_kernel(h_ref, g_ref, w_ref, o_ref):
    group_rows = RESIDUES_PER_GROUP * ATTN_BLOCK
    for part in range(RESIDUES_PER_STEP // RESIDUES_PER_GROUP):
        u = _rms(_gather_residues(h_ref, part * RESIDUES_PER_GROUP, RESIDUES_PER_GROUP), g_ref[...]).astype(BF16)
        o_ref[part * group_rows:(part + 1) * group_rows, :] = _mm(u, w_ref[...]).astype(o_ref.dtype)


def _qkv_proj(h, g, w, which):
    t, d = h.shape
    n = w.shape[-1]
    rows = RESIDUES_PER_STEP * ATTN_BLOCK
    per_span = MAX_DILATION // RESIDUES_PER_STEP
    view = h.reshape(t // SPAN, ATTN_BLOCK, MAX_DILATION, d)
    return pl.pallas_call(
        _qkv_kernel,
        out_shape=jax.ShapeDtypeStruct((t, n), BF16),
        grid=(t // rows,),
        in_specs=[pl.BlockSpec((None, ATTN_BLOCK, RESIDUES_PER_STEP, d),
                               lambda i: (i // per_span, 0, i % per_span, 0)),
                  _resident((1, d)), _stacked(w.shape, which)],
        out_specs=pl.BlockSpec((rows, n), lambda i: (i, 0)),
        compiler_params=_params("parallel"),
        name="qkv_proj",
    )(view, g.reshape(1, d), w)


HYB_IN_ROW_GROUPS = 4


def _hyb_in_kernel(h_ref, g_ref, w_ref, cw_ref, cb_ref, dtb_ref,
                   xbc_ref, glu_ref, zs_ref, dt_ref, xbuf_ref, *, tiles_per_seq):
    tm = h_ref.shape[0]
    s0 = 2 * glu_ref.shape[1]
    s1 = s0 + zs_ref.shape[1]
    s2 = s1 + xbc_ref.shape[1]
    wg_ref, wz_ref, wx_ref, wd_ref = w_ref.at[:, :s0], w_ref.at[:, s0:s1], w_ref.at[:, s1:s2], w_ref.at[:, s2:]
    i = pl.program_id(0)

    @pl.when(i % tiles_per_seq == 0)
    def _():
        xbuf_ref[0:SUBLANES, :] = jnp.zeros((SUBLANES, xbuf_ref.shape[1]), F32)

    @pl.when(i % tiles_per_seq != 0)
    def _():
        xbuf_ref[0:SUBLANES, :] = xbuf_ref[tm:tm + SUBLANES, :]

    rows_per_group = tm // HYB_IN_ROW_GROUPS
    first = SUBLANES - (SSM_CONV_WIDTH - 1)
    ch = glu_ref.shape[1]
    for part in range(HYB_IN_ROW_GROUPS):
        r0 = part * rows_per_group
        rows = slice(r0, r0 + rows_per_group)
        u = _rms(h_ref[rows, :], g_ref[...]).astype(BF16)
        xbuf_ref[SUBLANES + r0:SUBLANES + r0 + rows_per_group, :] = _mm(u, wx_ref[...])
        acc = jnp.broadcast_to(cb_ref[...], (rows_per_group, xbuf_ref.shape[1]))
        for k in range(SSM_CONV_WIDTH):
            acc = acc + xbuf_ref[r0 + first + k:r0 + first + k + rows_per_group, :] * cw_ref[k:k + 1, :]
        xbc_ref[rows, :] = _silu(acc)
        vg = _mm(u, wg_ref[...])
        glu_ref[rows, :] = vg[:, :ch] * jax.nn.sigmoid(vg[:, ch:])
        zs_ref[rows, :] = _silu(_mm(u, wz_ref[...]))
        dt_ref[rows, :] = jax.nn.softplus(_mm(u, wd_ref[...]) + dtb_ref[...])


def _hyb_in(h, g, w_in, which, conv_ch, d_inner, ssm_conv_w, ssm_conv_b, dt_bias, seq, tm=512):
    t, d = h.shape
    xbc_w = ssm_conv_w.shape[1]
    assert w_in.shape[-1] == 2 * conv_ch + d_inner + xbc_w + LANES
    cw = jnp.zeros((SUBLANES, xbc_w), F32).at[:SSM_CONV_WIDTH].set(ssm_conv_w)
    row = lambda n: pl.BlockSpec((tm, n), lambda i: (i, 0))
    widths = (xbc_w, conv_ch, d_inner, LANES)
    return pl.pallas_call(
        functools.partial(_hyb_in_kernel, tiles_per_seq=seq // tm),
        out_shape=[jax.ShapeDtypeStruct((t, n), F32) for n in widths],
        grid=(t // tm,),
        in_specs=[row(d), _resident((1, d)), _stacked(w_in.shape, which),
                  _resident((SUBLANES, xbc_w)), _resident((1, xbc_w)), _resident((1, LANES))],
        out_specs=[row(n) for n in widths],
        scratch_shapes=[pltpu.VMEM((tm + SUBLANES, xbc_w), F32)],
        compiler_params=_params("arbitrary"),
        name="hyb_in",
    )(h, g.reshape(1, d), w_in, cw, ssm_conv_b.reshape(1, xbc_w),
      jnp.pad(dt_bias, (0, LANES - SSM_HEADS)).reshape(1, LANES))


CONV_TAIL = 32
CONV_ROW_SPLIT = 4


def _convmod_kernel(glu_ref, w_ref, b_ref, lg_ref, lb_ref, o_ref, buf_ref, hid_ref, *, tl):
    ch = glu_ref.shape[-1]
    nt = ch // LANES

    @pl.when(pl.program_id(1) == 0)
    def _():
        buf_ref[0, :, 0:CONV_TAIL, :] = jnp.zeros((nt, CONV_TAIL, LANES), F32)

    @pl.when(pl.program_id(1) > 0)
    def _():
        buf_ref[0, :, 0:CONV_TAIL, :] = buf_ref[0, :, tl:tl + CONV_TAIL, :]

    for c in range(nt):
        buf_ref[0, c, CONV_TAIL:CONV_TAIL + tl, :] = glu_ref[:, c * LANES:(c + 1) * LANES]

    first = CONV_TAIL - (CONV_WIDTH - 1)
    shifted_rows = tl + CONV_TAIL - SUBLANES
    rows = tl // CONV_ROW_SPLIT
    tiles = (rows // SUBLANES, SUBLANES, LANES)

    def lane_tile(c, carry):
        for r in range(1, SUBLANES):
            buf_ref[r, c, 0:shifted_rows, :] = buf_ref[0, c, r:r + shifted_rows, :]
        wc = w_ref[c]
        wk = [jnp.broadcast_to(wc[k:k + 1, :], (SUBLANES, LANES))[None] for k in range(CONV_WIDTH)]
        for part in range(CONV_ROW_SPLIT):
            acc = jnp.broadcast_to(b_ref[c], tiles)
            for k in range(CONV_WIDTH):
                shift = (first + k) % SUBLANES
                row = part * rows + first + k - shift
                acc = acc + buf_ref[shift, c, row:row + rows, :].reshape(tiles) * wk[k]
            hid_ref[c, part * rows:(part + 1) * rows, :] = acc.reshape(rows, LANES)
        return carry

    lax.fori_loop(0, nt, lane_tile, 0)

    hid = jnp.concatenate([hid_ref[c] for c in range(nt)], axis=1)
    xc = hid - jnp.mean(hid, axis=-1, keepdims=True)
    y = xc * lax.rsqrt(jnp.mean(xc * xc, axis=-1, keepdims=True) + NORM_EPS)
    o_ref[...] = _silu(y * lg_ref[...] + lb_ref[...]).astype(o_ref.dtype)


def _convmod(glu, w, b, ln_g, ln_b, tl=512):
    bsz, seq, ch = glu.shape
    nt = ch // LANES
    wpad = jnp.zeros((CONV_TAIL, ch), F32).at[:CONV_WIDTH].set(w)
    wpad = wpad.reshape(CONV_TAIL, nt, LANES).transpose(1, 0, 2)
    vec = _resident((1, ch))
    return pl.pallas_call(
        functools.partial(_convmod_kernel, tl=tl),
        out_shape=jax.ShapeDtypeStruct((bsz, seq, ch), BF16),
        grid=(bsz, seq // tl),
        in_specs=[pl.BlockSpec((None, tl, ch), lambda bi, i: (bi, i, 0)),
                  _resident((nt, CONV_TAIL, LANES)), _resident((nt, 1, LANES)), vec, vec],
        out_specs=pl.BlockSpec((None, tl, ch), lambda bi, i: (bi, i, 0)),
        scratch_shapes=[pltpu.VMEM((SUBLANES, nt, tl + CONV_TAIL, LANES), F32), pltpu.VMEM((nt, tl, LANES), F32)],
        compiler_params=_params("parallel", "arbitrary"),
        name="convmod",
    )(glu, wpad, b.reshape(nt, 1, LANES), ln_g.reshape(1, ch), ln_b.reshape(1, ch))


SSD_CHUNKS_PER_STEP = 8


def _ssd_kernel(xbc_ref, zs_ref, dt_ref, alog_ref, dskip_ref, ng_ref, expand_ref, tri_ref, o_ref, state_ref):
    q = SSM_CHUNK
    nchunk = xbc_ref.shape[0] // q
    d_inner = SSM_HEADS * SSM_HEAD_DIM
    gcols = d_inner // SSM_GROUPS
    nst = SSM_STATE
    chunk = lambda c: slice(c * q, (c + 1) * q)

    @pl.when(pl.program_id(1) == 0)
    def _():
        state_ref[...] = jnp.zeros(state_ref.shape, F32)

    dt = dt_ref[...]
    a = dt * (-jnp.exp(alog_ref[...]))
    tri = tri_ref[...]
    a_hi = a.astype(BF16)
    a_mid = (a - a_hi.astype(F32)).astype(BF16)
    a_lo = (a - a_hi.astype(F32) - a_mid.astype(F32)).astype(BF16)
    acums = [_mm(tri, a_hi[chunk(c)]) + _mm(tri, a_mid[chunk(c)]) + _mm(tri, a_lo[chunk(c)])
             for c in range(nchunk)]
    acum = jnp.concatenate(acums, axis=0)
    dstate = jnp.concatenate([jnp.exp(ac[q - 1:q, :] - ac) for ac in acums], axis=0)
    per_head = jnp.concatenate([jnp.exp(acum), dstate * dt], axis=0)
    ph_hi = per_head.astype(BF16)
    ph_lo = (per_head - ph_hi.astype(F32)).astype(BF16)
    full = _mm(ph_hi, expand_ref[...]) + _mm(ph_lo, expand_ref[...])
    rows = nchunk * q
    eacum_full = full[0:rows]
    xs = xbc_ref[:, :d_inner]
    xw = (xs * full[rows:2 * rows]).astype(BF16)
    causal = lax.broadcasted_iota(jnp.int32, (q, q), 0) >= lax.broadcasted_iota(jnp.int32, (q, q), 1)
    low_half = lax.broadcasted_iota(jnp.int32, (q, LANES), 1) < SSM_HEAD_DIM
    heads_per_group = SSM_HEADS // SSM_GROUPS

    state = state_ref[...]
    for c in range(nchunk):
        r = chunk(c)
        ac = acums[c]
        ac_t = ac.T
        dt_t = dt[r].T
        y_diag, y_off, new_states = [], [], []
        for g in range(SSM_GROUPS):
            bg = xbc_ref[r, d_inner + g * nst:d_inner + (g + 1) * nst].astype(BF16)
            cg = xbc_ref[r, d_inner + (SSM_GROUPS + g) * nst:d_inner + (SSM_GROUPS + g + 1) * nst].astype(BF16)
            gsl = slice(g * gcols, (g + 1) * gcols)
            cb = lax.dot_general(cg, bg, (((1,), (1,)), ((), ())), preferred_element_type=F32)
            y_off.append(_mm(cg, state[:, gsl].astype(BF16)))
            new_states.append(lax.dot_general(bg, xw[r, gsl], (((0,), (0,)), ((), ())),
                                              preferred_element_type=F32))
            for pair in range(heads_per_group // 2):
                h0 = g * heads_per_group + 2 * pair
                masks = []
                for h in (h0, h0 + 1):
                    diff = ac[:, h:h + 1] - ac_t[h:h + 1, :]
                    decay_in = jnp.exp(jnp.where(causal, diff, MASK_VALUE)) * dt_t[h:h + 1, :]
                    masks.append((cb * decay_in).astype(BF16))
                lhs = jnp.concatenate(masks, axis=1)
                xp = xs[r, h0 * SSM_HEAD_DIM:(h0 + 2) * SSM_HEAD_DIM]
                rhs = jnp.concatenate([jnp.where(low_half, xp, 0.0), jnp.where(low_half, 0.0, xp)],
                                      axis=0).astype(BF16)
                y_diag.append(_mm(lhs, rhs))
        decay = eacum_full[r]
        y = (jnp.concatenate(y_diag, axis=1) + jnp.concatenate(y_off, axis=1) * decay
             + dskip_ref[...] * xs[r])
        state = state * decay[q - 1:q, :] + jnp.concatenate(new_states, axis=1)
        y = y * zs_ref[r, :]
        o_ref[r, :] = _rms(y, ng_ref[...]).astype(o_ref.dtype)
    state_ref[...] = state


def _ssd(xbc, zs, dt, a_log, d_skip, norm_g):
    bsz, seq, xbc_w = xbc.shape
    q = SSM_CHUNK
    rows = SSD_CHUNKS_PER_STEP * q
    d_inner = SSM_HEADS * SSM_HEAD_DIM
    head_of_col = jnp.arange(d_inner) // SSM_HEAD_DIM
    expand = (jnp.arange(LANES)[:, None] == head_of_col[None, :]).astype(BF16)
    tri = (jnp.arange(q)[:, None] >= jnp.arange(q)[None, :]).astype(BF16)
    return pl.pallas_call(
        _ssd_kernel,
        out_shape=jax.ShapeDtypeStruct((bsz, seq, d_inner), BF16),
        grid=(bsz, seq // rows),
        in_specs=[pl.BlockSpec((None, rows, xbc_w), lambda bi, i: (bi, i, 0)),
                  pl.BlockSpec((None, rows, d_inner), lambda bi, i: (bi, i, 0)),
                  pl.BlockSpec((None, rows, LANES), lambda bi, i: (bi, i, 0)),
                  _resident((1, LANES)), _resident((1, d_inner)), _resident((1, d_inner)),
                  _resident((LANES, d_inner)), _resident((q, q))],
        out_specs=pl.BlockSpec((None, rows, d_inner), lambda bi, i: (bi, i, 0)),
        scratch_shapes=[pltpu.VMEM((SSM_STATE, d_inner), F32)],
        compiler_params=_params("parallel", "arbitrary"),
        name="ssd",
    )(xbc, zs, dt, jnp.pad(a_log, (0, LANES - SSM_HEADS)).reshape(1, LANES),
      jnp.repeat(d_skip, SSM_HEAD_DIM).reshape(1, d_inner), norm_g.reshape(1, d_inner), expand, tri)


def _proj_res_kernel(h_ref, xa_ref, xb_ref, w_ref, g_ref, o_ref):
    ka = xa_ref.shape[1]
    y = _mm(xa_ref[...], w_ref[:ka, :]) + _mm(xb_ref[...], w_ref[ka:, :])
    o_ref[...] = h_ref[...] + _rms(y, g_ref[...])


def _proj_res(h, xa, xb, w, which, g, tm=512):
    t, d = h.shape
    ka, kb = xa.shape[1], xb.shape[1]
    assert w.shape[-2] == ka + kb
    row = lambda n: pl.BlockSpec((tm, n), lambda i: (i, 0))
    return pl.pallas_call(
        _proj_res_kernel,
        out_shape=jax.ShapeDtypeStruct((t, d), F32),
        grid=(t // tm,),
        in_specs=[row(d), row(ka), row(kb), _stacked(w.shape, which), _resident((1, d))],
        out_specs=row(d),
        compiler_params=_params("parallel"),
        name="proj_res",
    )(h, xa, xb, w, g.reshape(1, d))


MAX_DILATION = 16
SPAN = ATTN_BLOCK * MAX_DILATION
ATTN_WIDTH = ATTN_HEADS * ATTN_HEAD_DIM


def _attend(jobs, bias_ref, scores_ref):
    dh = ATTN_HEAD_DIM
    nq = ATTN_BLOCK
    npairs = ATTN_HEADS // 2
    low_head = lax.broadcasted_iota(jnp.int32, (nq, LANES), 1) < dh
    for j, (q, k, _, _) in enumerate(jobs):
        for pair in range(npairs):
            cols = slice(pair * LANES, (pair + 1) * LANES)
            qp = q[:, cols]
            q2 = jnp.concatenate([jnp.where(low_head, qp, 0), jnp.where(low_head, 0, qp)], axis=0)
            scores_ref[j * npairs + pair] = lax.dot_general(k[:, cols], q2, (((1,), (1,)), ((), ())),
                                                            preferred_element_type=F32)
    results = []
    for j, (_, _, v, table) in enumerate(jobs):
        outs, lses = [], []
        for pair in range(npairs):
            cols = slice(pair * LANES, (pair + 1) * LANES)
            st = scores_ref[j * npairs + pair] + bias_ref[table, pair]
            m = jnp.max(st, axis=0, keepdims=True)
            p = jnp.exp(st - m)
            l = jnp.sum(p, axis=0, keepdims=True)
            ot = lax.dot_general(v[:, cols], p.astype(BF16), (((0,), (0,)), ((), ())), preferred_element_type=F32)
            ot = ot / l
            lse = m + jnp.log(l)
            for hh in range(2):
                outs.append(ot[hh * dh:(hh + 1) * dh, hh * nq:(hh + 1) * nq])
                lses.append(lse[:, hh * nq:(hh + 1) * nq])
        lse_t = jnp.concatenate(lses + [jnp.zeros((LANES - ATTN_HEADS, nq), F32)], axis=0)
        results.append((jnp.concatenate(outs, axis=0).T, lse_t.T))
    return results


MAX_BLOCKS_PER_STEP = 16


def _reset_carry(kprev_ref, vprev_ref):
    @pl.when(pl.program_id(2) == 0)
    def _():
        kprev_ref[...] = jnp.zeros(kprev_ref.shape, kprev_ref.dtype)
        vprev_ref[...] = jnp.zeros(vprev_ref.shape, vprev_ref.dtype)


def _attn_pairs_kernel(q_ref, kprev_ref, k_ref, vprev_ref, v_ref, bias_ref, o_ref, lse_ref, scores_ref):
    table = jnp.minimum(pl.program_id(2), 1)
    scale = ATTN_HEAD_DIM ** -0.5
    blocks = [slice(b * ATTN_BLOCK, (b + 1) * ATTN_BLOCK) for b in range(q_ref.shape[0] // ATTN_BLOCK)]
    jobs = [(q_ref[rows, :] * scale,
             jnp.concatenate([kprev_ref[rows, :], k_ref[rows, :]], axis=0),
             jnp.concatenate([vprev_ref[rows, :], v_ref[rows, :]], axis=0), table) for rows in blocks]
    for rows, (o, lse) in zip(blocks, _attend(jobs, bias_ref, scores_ref)):
        o_ref[rows, :] = o.astype(o_ref.dtype)
        lse_ref[rows, :] = lse


def _attn_chain_kernel(q_ref, k_ref, v_ref, bias_ref, o_ref, lse_ref, scores_ref, kprev_ref, vprev_ref, *, runs,
                       run_rows):
    _reset_carry(kprev_ref, vprev_ref)
    nb = runs * run_rows // ATTN_BLOCK
    part = run_rows // nb
    scale = ATTN_HEAD_DIM ** -0.5

    def parts(ref):
        x = ref[...].astype(F32)
        return [x[:, p * part:(p + 1) * part, :].reshape(ATTN_BLOCK, x.shape[-1])
                for p in range(nb)]

    k32, v32 = parts(k_ref), parts(v_ref)
    qs = [x.astype(BF16) for x in parts(q_ref)]
    ks = [x.astype(BF16) for x in k32]
    vs = [x.astype(BF16) for x in v32]
    kprev = [kprev_ref[...].astype(BF16)] + ks[:-1]
    vprev = [vprev_ref[...].astype(BF16)] + vs[:-1]
    first_table = jnp.minimum(pl.program_id(2), 1)
    jobs = [(qs[p] * scale, jnp.concatenate([kprev[p], ks[p]], axis=0),
             jnp.concatenate([vprev[p], vs[p]], axis=0), first_table if p == 0 else 1)
            for p in range(nb)]
    results = _attend(jobs, bias_ref, scores_ref)

    def interleave(xs):
        return jnp.concatenate([x.reshape(runs, part, x.shape[-1]) for x in xs], axis=1)

    o_ref[...] = interleave([o for o, _ in results]).astype(o_ref.dtype)
    lse_ref[...] = interleave([lse for _, lse in results])
    kprev_ref[...] = k32[-1]
    vprev_ref[...] = v32[-1]


def _attn_bias(window, dilation, offsets):
    steps = window // dilation
    slopes = jnp.exp2(-8.0 * jnp.arange(1, ATTN_HEADS + 1, dtype=F32) / ATTN_HEADS)
    kj = jnp.concatenate([offsets - ATTN_BLOCK, offsets])[:, None]
    qi = offsets[None, :]
    rel = qi - kj
    valid = (rel >= 0) & (rel <= steps)
    bias = -slopes[:, None, None] * (rel * dilation).astype(F32)
    later = jnp.where(valid[None], bias, MASK_VALUE)
    first = jnp.where((valid & (kj >= 0))[None], bias, MASK_VALUE)
    table = jnp.stack([first, later], axis=0)
    table = table.reshape(2, ATTN_HEADS // 2, 2, 2 * ATTN_BLOCK, ATTN_BLOCK)
    return table.transpose(0, 1, 3, 2, 4).reshape(2, ATTN_HEADS // 2, 2 * ATTN_BLOCK, 2 * ATTN_BLOCK)


def _attn_group(qkv, g, bsz, seq):
    window, dilation = DILATED_PATTERNS[g]
    ngroups = len(DILATED_PATTERNS)
    assert window // dilation <= ATTN_BLOCK and seq % SPAN == 0 and MAX_DILATION % dilation == 0
    t = bsz * seq
    nspan = seq // SPAN
    ncols = qkv.shape[1]
    col = lambda which: which * ngroups + g
    bias_spec = pl.BlockSpec((2, ATTN_HEADS // 2, 2 * ATTN_BLOCK, 2 * ATTN_BLOCK), lambda *_: (0, 0, 0, 0),
                             pipeline_mode=pl.Buffered(1))
    u = jnp.arange(ATTN_BLOCK)
    nb = min(MAX_BLOCKS_PER_STEP, MAX_DILATION // dilation if dilation < MAX_DILATION else MAX_BLOCKS_PER_STEP)
    scores_scratch = pltpu.VMEM((nb * ATTN_HEADS // 2, 2 * ATTN_BLOCK, 2 * ATTN_BLOCK), F32)

    if dilation == MAX_DILATION:
        groups = MAX_DILATION // nb
        rows = nb * ATTN_BLOCK
        cur = lambda which: pl.BlockSpec((rows, ATTN_WIDTH),
                                         lambda b, r, j: ((b * nspan + j) * groups + r, col(which)))
        prev = lambda which: pl.BlockSpec(
            (rows, ATTN_WIDTH), lambda b, r, j: ((b * nspan + jnp.maximum(j - 1, 0)) * groups + r, col(which)))
        out = lambda w: pl.BlockSpec((rows, w), lambda b, r, j: ((b * nspan + j) * groups + r, 0))
        o, lse = pl.pallas_call(
            _attn_pairs_kernel,
            out_shape=[jax.ShapeDtypeStruct((t, ATTN_WIDTH), BF16), jax.ShapeDtypeStruct((t, LANES), F32)],
            grid=(bsz, groups, nspan),
            in_specs=[cur(0), prev(1), cur(1), prev(2), cur(2), bias_spec],
            out_specs=[out(ATTN_WIDTH), out(LANES)],
            scratch_shapes=[scores_scratch],
            compiler_params=_params("parallel", "parallel", "arbitrary"),
            name=f"attn_d{dilation}",
        )(qkv, qkv, qkv, qkv, qkv, _attn_bias(window, dilation, u))
        return o, lse

    runs = MAX_DILATION // dilation
    run_rows = nb * ATTN_BLOCK // runs
    steps_per_span = ATTN_BLOCK // run_rows
    shape = (bsz * nspan, runs, dilation, steps_per_span, run_rows)
    blk = lambda w: (None, runs, None, None, run_rows, w)
    at = lambda b, r, s, c: (b * nspan + s // steps_per_span, 0, r, s % steps_per_span, 0, c)
    cur = lambda which: pl.BlockSpec(blk(ATTN_WIDTH), lambda b, r, s: at(b, r, s, col(which)))
    out = lambda w: pl.BlockSpec(blk(w), lambda b, r, s: at(b, r, s, 0))
    view = qkv.reshape(shape + (ncols,))
    part = run_rows // nb
    offsets = runs * (u % part) + u // part
    carry = pltpu.VMEM((ATTN_BLOCK, ATTN_WIDTH), F32)
    o, lse = pl.pallas_call(
        functools.partial(_attn_chain_kernel, runs=runs, run_rows=run_rows),
        out_shape=[jax.ShapeDtypeStruct(shape + (ATTN_WIDTH,), BF16), jax.ShapeDtypeStruct(shape + (LANES,), F32)],
        grid=(bsz, dilation, nspan * steps_per_span),
        in_specs=[cur(0), cur(1), cur(2), bias_spec],
        out_specs=[out(ATTN_WIDTH), out(LANES)],
        scratch_shapes=[scores_scratch, carry, carry],
        compiler_params=_params("parallel", "parallel", "arbitrary"),
        name=f"attn_d{dilation}",
    )(view, view, view, _attn_bias(window, dilation, offsets))
    return o.reshape(t, ATTN_WIDTH), lse.reshape(t, LANES)


def _merge_proj_kernel(h_ref, o0_ref, o1_ref, o2_ref, l0_ref, l1_ref, l2_ref, expand_ref, w_ref, g_ref, out_ref):
    l0, l1, l2 = l0_ref[...], l1_ref[...], l2_ref[...]
    m = jnp.maximum(jnp.maximum(l0, l1), l2)
    e0, e1, e2 = jnp.exp(l0 - m), jnp.exp(l1 - m), jnp.exp(l2 - m)
    inv = 1.0 / (e0 + e1 + e2)
    merged = None
    for e, o_ref in ((e0, o0_ref), (e1, o1_ref), (e2, o2_ref)):
        wgt = e * inv
        w_hi = wgt.astype(BF16)
        w_lo = (wgt - w_hi.astype(F32)).astype(BF16)
        term = (_mm(w_hi, expand_ref[...]) + _mm(w_lo, expand_ref[...])) * o_ref[...]
        merged = term if merged is None else merged + term
    y = _rms(_mm(merged.astype(BF16), w_ref[...]), g_ref[...])
    for r in range(out_ref.shape[1]):
        out_ref[:, r, :] = y[r * ATTN_BLOCK:(r + 1) * ATTN_BLOCK, :]
    out_ref[...] = out_ref[...] + h_ref[...]


def _merge_proj(h, outs, lses, w, which, g):
    t, d = h.shape
    k = w.shape[-2]
    rows = RESIDUES_PER_STEP * ATTN_BLOCK
    per_span = MAX_DILATION // RESIDUES_PER_STEP
    row = lambda n: pl.BlockSpec((rows, n), lambda i: (i, 0))
    tokens = pl.BlockSpec((None, ATTN_BLOCK, RESIDUES_PER_STEP, d), lambda i: (i // per_span, 0, i % per_span, 0))
    head_of_col = jnp.arange(k) // ATTN_HEAD_DIM
    expand = (jnp.arange(LANES)[:, None] == head_of_col[None, :]).astype(BF16)
    view = (t // SPAN, ATTN_BLOCK, MAX_DILATION, d)
    return pl.pallas_call(
        _merge_proj_kernel,
        out_shape=jax.ShapeDtypeStruct(view, F32),
        grid=(t // rows,),
        in_specs=[tokens] + [row(k)] * 3 + [row(LANES)] * 3
        + [_resident((LANES, k)), _stacked(w.shape, which), _resident((1, d))],
        out_specs=tokens,
        compiler_params=_params("parallel"),
        name="merge_proj",
    )(h.reshape(view), *outs, *lses, expand, w, g.reshape(1, d)).reshape(t, d)


def _conv_ssd_mixer(h, bsz, seq, g_pre, g_post, w_in, j, conv_dw_w, conv_dw_b, conv_ln_g, conv_ln_b,
                    ssm_conv_w, ssm_conv_b, ssm_dt_bias, ssm_a_log, ssm_d, ssm_norm_g, w_out):
    conv_ch = conv_dw_w.shape[1]
    d_inner = SSM_HEADS * SSM_HEAD_DIM
    xbc_w = ssm_conv_w.shape[1]
    xbc, glu, zs, dt = _hyb_in(h, g_pre, w_in, (j,), conv_ch, d_inner, ssm_conv_w, ssm_conv_b, ssm_dt_bias, seq)
    ya = _convmod(glu.reshape(bsz, seq, conv_ch), conv_dw_w, conv_dw_b, conv_ln_g, conv_ln_b)
    yb = _ssd(xbc.reshape(bsz, seq, xbc_w), zs.reshape(bsz, seq, d_inner), dt.reshape(bsz, seq, LANES),
              ssm_a_log, ssm_d, ssm_norm_g)
    t = bsz * seq
    return _proj_res(h, ya.reshape(t, conv_ch), yb.reshape(t, d_inner), w_out, (j,), g_post)


def _attention_mixer(h, bsz, seq, g_pre, g_post, w_qkv, w_o, j):
    qkv = _qkv_proj(h, g_pre, w_qkv, (j,))
    outs, lses = [], []
    for g in range(len(DILATED_PATTERNS)):
        o, lse = _attn_group(qkv, g, bsz, seq)
        outs.append(o)
        lses.append(lse)
    return _merge_proj(h, outs, lses, w_o, (j,), g_post)


def kernel(x, norm_g, ffn_w1, ffn_w2, hyb_w_in, conv_dw_w, conv_dw_b, conv_ln_g, conv_ln_b, ssm_conv_w, ssm_conv_b, ssm_dt_bias, ssm_a_log, ssm_d, ssm_norm_g, hyb_w_out, attn_w_qkv, attn_w_o):
    bsz, seq, d = x.shape
    h = x.reshape(bsz * seq, d)
    w1, w2 = ffn_w1.astype(BF16), ffn_w2.astype(BF16)
    w_in = jnp.pad(hyb_w_in, ((0, 0), (0, 0), (0, LANES - SSM_HEADS))).astype(BF16)
    w_out, w_qkv, w_o = hyb_w_out.astype(BF16), attn_w_qkv.astype(BF16), attn_w_o.astype(BF16)
    for i in range(norm_g.shape[0]):
        g = norm_g[i]
        j = i // 2
        h = _ffn(h, g[0], g[1], w1, w2, (i, 0))
        if i % 2 == 0:
            h = _conv_ssd_mixer(h, bsz, seq, g[2], g[3], w_in, j, conv_dw_w[j], conv_dw_b[j],
                                conv_ln_g[j], conv_ln_b[j], ssm_conv_w[j], ssm_conv_b[j], ssm_dt_bias[j],
                                ssm_a_log[j], ssm_d[j], ssm_norm_g[j], w_out)
        else:
            h = _attention_mixer(h, bsz, seq, g[2], g[3], w_qkv, w_o, j)
        h = _ffn(h, g[4], g[5], w1, w2, (i, 1))
    return h.reshape(bsz, seq, d)
```

```python
import functools

import jax
import jax.numpy as jnp
from jax import lax
from jax.experimental import pallas as pl
from jax.experimental.pallas import tpu as pltpu

NORM_EPS = 1e-6
CONV_WIDTH = 31
SSM_HEAD_DIM = 64
SSM_HEADS = 16
SSM_GROUPS = 2
SSM_STATE = 128
SSM_CONV_WIDTH = 4
SSM_CHUNK = 128
ATTN_HEAD_DIM = 64
ATTN_HEADS = 8
DILATED_PATTERNS = ((128, 1), (512, 4), (2048, 16))
ATTN_BLOCK = 128

LANES = 128
SUBLANES = 8
VMEM_LIMIT_BYTES = 56 * 1024 * 1024
MASK_VALUE = -1e30

F32 = jnp.float32
BF16 = jnp.bfloat16


def _params(*sem):
    return pltpu.CompilerParams(dimension_semantics=sem, vmem_limit_bytes=VMEM_LIMIT_BYTES)


def _resident(shape):
    nd = len(shape)
    return pl.BlockSpec(shape, lambda *_: (0,) * nd, pipeline_mode=pl.Buffered(1))


def _stacked(shape, which):
    lead = len(which)
    nd = len(shape)
    return pl.BlockSpec((None,) * lead + tuple(shape[lead:]), lambda *_: tuple(which) + (0,) * (nd - lead),
                        pipeline_mode=pl.Buffered(1))


def _rms(x, g):
    return x * lax.rsqrt(jnp.mean(x * x, axis=-1, keepdims=True) + NORM_EPS) * g


def _silu(x):
    return x * jax.nn.sigmoid(x)


def _mm(a, b):
    return jnp.dot(a, b, preferred_element_type=F32)


FFN_ROW_GROUPS = 4


def _ffn_kernel(h_ref, gpre_ref, gpost_ref, w1_ref, w2_ref, o_ref):
    f = w2_ref.shape[0]
    half = h_ref.shape[0] // FFN_ROW_GROUPS
    for part in range(FFN_ROW_GROUPS):
        rows = slice(part * half, (part + 1) * half)
        h = h_ref[rows, :]
        u = _rms(h, gpre_ref[...]).astype(BF16)
        gate = _mm(u, w1_ref[:, :f])
        up = _mm(u, w1_ref[:, f:])
        act = (_silu(gate) * up).astype(BF16)
        y = _mm(act, w2_ref[...])
        o_ref[rows, :] = h + 0.5 * _rms(y, gpost_ref[...])


def _ffn(h, g_pre, g_post, w1, w2, which, tm=512):
    t, d = h.shape
    f = w2.shape[-2]
    row = pl.BlockSpec((tm, d), lambda i: (i, 0))
    return pl.pallas_call(
        _ffn_kernel,
        out_shape=jax.ShapeDtypeStruct((t, d), F32),
        grid=(t // tm,),
        in_specs=[row, _resident((1, d)), _resident((1, d)),
                  _stacked(w1.shape, which), _stacked(w2.shape, which)],
        out_specs=row,
        compiler_params=_params("parallel"),
        name="ffn",
    )(h, g_pre.reshape(1, d), g_post.reshape(1, d), w1, w2)


RESIDUES_PER_STEP = SUBLANES


RESIDUES_PER_GROUP = 4


def _gather_residues(h_ref, r0, count):
    return jnp.concatenate([h_ref[:, r, :] for r in range(r0, r0 + count)], axis=0)


def _qkv_kernel(h_ref, g_ref, w_ref, o_ref):
    group_rows = RESIDUES_PER_GROUP * ATTN_BLOCK
    for part in range(RESIDUES_PER_STEP // RESIDUES_PER_GROUP):
        u = _rms(_gather_residues(h_ref, part * RESIDUES_PER_GROUP, RESIDUES_PER_GROUP), g_ref[...]).astype(BF16)
        o_ref[part * group_rows:(part + 1) * group_rows, :] = _mm(u, w_ref[...]).astype(o_ref.dtype)


def _qkv_proj(h, g, w, which):
    t, d = h.shape
    n = w.shape[-1]
    rows = RESIDUES_PER_STEP * ATTN_BLOCK
    per_span = MAX_DILATION // RESIDUES_PER_STEP
    view = h.reshape(t // SPAN, ATTN_BLOCK, MAX_DILATION, d)
    return pl.pallas_call(
        _qkv_kernel,
        out_shape=jax.ShapeDtypeStruct((t, n), BF16),
        grid=(t // rows,),
        in_specs=[pl.BlockSpec((None, ATTN_BLOCK, RESIDUES_PER_STEP, d),
                               lambda i: (i // per_span, 0, i % per_span, 0)),
                  _resident((1, d)), _stacked(w.shape, which)],
        out_specs=pl.BlockSpec((rows, n), lambda i: (i, 0)),
        compiler_params=_params("parallel"),
        name="qkv_proj",
    )(view, g.reshape(1, d), w)


HYB_IN_ROW_GROUPS = 4


def _hyb_in_kernel(h_ref, g_ref, w_ref, cw_ref, cb_ref, dtb_ref,
                   xbc_ref, glu_ref, zs_ref, dt_ref, xbuf_ref, *, tiles_per_seq):
    tm = h_ref.shape[0]
    s0 = 2 * glu_ref.shape[1]
    s1 = s0 + zs_ref.shape[1]
    s2 = s1 + xbc_ref.shape[1]
    wg_ref, wz_ref, wx_ref, wd_ref = w_ref.at[:, :s0], w_ref.at[:, s0:s1], w_ref.at[:, s1:s2], w_ref.at[:, s2:]
    i = pl.program_id(0)

    @pl.when(i % tiles_per_seq == 0)
    def _():
        xbuf_ref[0:SUBLANES, :] = jnp.zeros((SUBLANES, xbuf_ref.shape[1]), F32)

    @pl.when(i % tiles_per_seq != 0)
    def _():
        xbuf_ref[0:SUBLANES, :] = xbuf_ref[tm:tm + SUBLANES, :]

    rows_per_group = tm // HYB_IN_ROW_GROUPS
    first = SUBLANES - (SSM_CONV_WIDTH - 1)
    ch = glu_ref.shape[1]
    for part in range(HYB_IN_ROW_GROUPS):
        r0 = part * rows_per_group
        rows = slice(r0, r0 + rows_per_group)
        u = _rms(h_ref[rows, :], g_ref[...]).astype(BF16)
        xbuf_ref[SUBLANES + r0:SUBLANES + r0 + rows_per_group, :] = _mm(u, wx_ref[...])
        acc = jnp.broadcast_to(cb_ref[...], (rows_per_group, xbuf_ref.shape[1]))
        for k in range(SSM_CONV_WIDTH):
            acc = acc + xbuf_ref[r0 + first + k:r0 + first + k + rows_per_group, :] * cw_ref[k:k + 1, :]
        xbc_ref[rows, :] = _silu(acc)
        vg = _mm(u, wg_ref[...])
        glu_ref[rows, :] = vg[:, :ch] * jax.nn.sigmoid(vg[:, ch:])
        zs_ref[rows, :] = _silu(_mm(u, wz_ref[...]))
        dt_ref[rows, :] = jax.nn.softplus(_mm(u, wd_ref[...]) + dtb_ref[...])


def _hyb_in(h, g, w_in, which, conv_ch, d_inner, ssm_conv_w, ssm_conv_b, dt_bias, seq, tm=512):
    t, d = h.shape
    xbc_w = ssm_conv_w.shape[1]
    assert w_in.shape[-1] == 2 * conv_ch + d_inner + xbc_w + LANES
    cw = jnp.zeros((SUBLANES, xbc_w), F32).at[:SSM_CONV_WIDTH].set(ssm_conv_w)
    row = lambda n: pl.BlockSpec((tm, n), lambda i: (i, 0))
    widths = (xbc_w, conv_ch, d_inner, LANES)
    return pl.pallas_call(
        functools.partial(_hyb_in_kernel, tiles_per_seq=seq // tm),
        out_shape=[jax.ShapeDtypeStruct((t, n), F32) for n in widths],
        grid=(t // tm,),
        in_specs=[row(d), _resident((1, d)), _stacked(w_in.shape, which),
                  _resident((SUBLANES, xbc_w)), _resident((1, xbc_w)), _resident((1, LANES))],
        out_specs=[row(n) for n in widths],
        scratch_shapes=[pltpu.VMEM((tm + SUBLANES, xbc_w), F32)],
        compiler_params=_params("arbitrary"),
        name="hyb_in",
    )(h, g.reshape(1, d), w_in, cw, ssm_conv_b.reshape(1, xbc_w),
      jnp.pad(dt_bias, (0, LANES - SSM_HEADS)).reshape(1, LANES))


CONV_TAIL = 32
CONV_ROW_SPLIT = 4


def _convmod_kernel(glu_ref, w_ref, b_ref, lg_ref, lb_ref, o_ref, buf_ref, hid_ref, *, tl):
    ch = glu_ref.shape[-1]
    nt = ch // LANES

    @pl.when(pl.program_id(1) == 0)
    def _():
        buf_ref[0, :, 0:CONV_TAIL, :] = jnp.zeros((nt, CONV_TAIL, LANES), F32)

    @pl.when(pl.program_id(1) > 0)
    def _():
        buf_ref[0, :, 0:CONV_TAIL, :] = buf_ref[0, :, tl:tl + CONV_TAIL, :]

    for c in range(nt):
        buf_ref[0, c, CONV_TAIL:CONV_TAIL + tl, :] = glu_ref[:, c * LANES:(c + 1) * LANES]

    first = CONV_TAIL - (CONV_WIDTH - 1)
    shifted_rows = tl + CONV_TAIL - SUBLANES
    rows = tl // CONV_ROW_SPLIT
    tiles = (rows // SUBLANES, SUBLANES, LANES)

    def lane_tile(c, carry):
        for r in range(1, SUBLANES):
            buf_ref[r, c, 0:shifted_rows, :] = buf_ref[0, c, r:r + shifted_rows, :]
        wc = w_ref[c]
        wk = [jnp.broadcast_to(wc[k:k + 1, :], (SUBLANES, LANES))[None] for k in range(CONV_WIDTH)]
        for part in range(CONV_ROW_SPLIT):
            acc = jnp.broadcast_to(b_ref[c], tiles)
            for k in range(CONV_WIDTH):
                shift = (first + k) % SUBLANES
                row = part * rows + first + k - shift
                acc = acc + buf_ref[shift, c, row:row + rows, :].reshape(tiles) * wk[k]
            hid_ref[c, part * rows:(part + 1) * rows, :] = acc.reshape(rows, LANES)
        return carry

    lax.fori_loop(0, nt, lane_tile, 0)

    hid = jnp.concatenate([hid_ref[c] for c in range(nt)], axis=1)
    xc = hid - jnp.mean(hid, axis=-1, keepdims=True)
    y = xc * lax.rsqrt(jnp.mean(xc * xc, axis=-1, keepdims=True) + NORM_EPS)
    o_ref[...] = _silu(y * lg_ref[...] + lb_ref[...]).astype(o_ref.dtype)


def _convmod(glu, w, b, ln_g, ln_b, tl=512):
    bsz, seq, ch = glu.shape
    nt = ch // LANES
    wpad = jnp.zeros((CONV_TAIL, ch), F32).at[:CONV_WIDTH].set(w)
    wpad = wpad.reshape(CONV_TAIL, nt, LANES).transpose(1, 0, 2)
    vec = _resident((1, ch))
    return pl.pallas_call(
        functools.partial(_convmod_kernel, tl=tl),
        out_shape=jax.ShapeDtypeStruct((bsz, seq, ch), BF16),
        grid=(bsz, seq // tl),
        in_specs=[pl.BlockSpec((None, tl, ch), lambda bi, i: (bi, i, 0)),
                  _resident((nt, CONV_TAIL, LANES)), _resident((nt, 1, LANES)), vec, vec],
        out_specs=pl.BlockSpec((None, tl, ch), lambda bi, i: (bi, i, 0)),
        scratch_shapes=[pltpu.VMEM((SUBLANES, nt, tl + CONV_TAIL, LANES), F32), pltpu.VMEM((nt, tl, LANES), F32)],
        compiler_params=_params("parallel", "arbitrary"),
        name="convmod",
    )(glu, wpad, b.reshape(nt, 1, LANES), ln_g.reshape(1, ch), ln_b.reshape(1, ch))


SSD_CHUNKS_PER_STEP = 8


def _ssd_kernel(xbc_ref, zs_ref, dt_ref, alog_ref, dskip_ref, ng_ref, expand_ref, tri_ref, o_ref, state_ref):
    q = SSM_CHUNK
    nchunk = xbc_ref.shape[0] // q
    d_inner = SSM_HEADS * SSM_HEAD_DIM
    gcols = d_inner // SSM_GROUPS
    nst = SSM_STATE
    chunk = lambda c: slice(c * q, (c + 1) * q)

    @pl.when(pl.program_id(1) == 0)
    def _():
        state_ref[...] = jnp.zeros(state_ref.shape, F32)

    dt = dt_ref[...]
    a = dt * (-jnp.exp(alog_ref[...]))
    tri = tri_ref[...]
    a_hi = a.astype(BF16)
    a_mid = (a - a_hi.astype(F32)).astype(BF16)
    a_lo = (a - a_hi.astype(F32) - a_mid.astype(F32)).astype(BF16)
    acums = [_mm(tri, a_hi[chunk(c)]) + _mm(tri, a_mid[chunk(c)]) + _mm(tri, a_lo[chunk(c)])
             for c in range(nchunk)]
    acum = jnp.concatenate(acums, axis=0)
    dstate = jnp.concatenate([jnp.exp(ac[q - 1:q, :] - ac) for ac in acums], axis=0)
    per_head = jnp.concatenate([jnp.exp(acum), dstate * dt], axis=0)
    ph_hi = per_head.astype(BF16)
    ph_lo = (per_head - ph_hi.astype(F32)).astype(BF16)
    full = _mm(ph_hi, expand_ref[...]) + _mm(ph_lo, expand_ref[...])
    rows = nchunk * q
    eacum_full = full[0:rows]
    xs = xbc_ref[:, :d_inner]
    xw = (xs * full[rows:2 * rows]).astype(BF16)
    causal = lax.broadcasted_iota(jnp.int32, (q, q), 0) >= lax.broadcasted_iota(jnp.int32, (q, q), 1)
    low_half = lax.broadcasted_iota(jnp.int32, (q, LANES), 1) < SSM_HEAD_DIM
    heads_per_group = SSM_HEADS // SSM_GROUPS

    state = state_ref[...]
    for c in range(nchunk):
        r = chunk(c)
        ac = acums[c]
        ac_t = ac.T
        dt_t = dt[r].T
        y_diag, y_off, new_states = [], [], []
        for g in range(SSM_GROUPS):
            bg = xbc_ref[r, d_inner + g * nst:d_inner + (g + 1) * nst].astype(BF16)
            cg = xbc_ref[r, d_inner + (SSM_GROUPS + g) * nst:d_inner + (SSM_GROUPS + g + 1) * nst].astype(BF16)
            gsl = slice(g * gcols, (g + 1) * gcols)
            cb = lax.dot_general(cg, bg, (((1,), (1,)), ((), ())), preferred_element_type=F32)
            y_off.append(_mm(cg, state[:, gsl].astype(BF16)))
            new_states.append(lax.dot_general(bg, xw[r, gsl], (((0,), (0,)), ((), ())),
                                              preferred_element_type=F32))
            for pair in range(heads_per_group // 2):
                h0 = g * heads_per_group + 2 * pair
                masks = []
                for h in (h0, h0 + 1):
                    diff = ac[:, h:h + 1] - ac_t[h:h + 1, :]
                    decay_in = jnp.exp(jnp.where(causal, diff, MASK_VALUE)) * dt_t[h:h + 1, :]
                    masks.append((cb * decay_in).astype(BF16))
                lhs = jnp.concatenate(masks, axis=1)
                xp = xs[r, h0 * SSM_HEAD_DIM:(h0 + 2) * SSM_HEAD_DIM]
                rhs = jnp.concatenate([jnp.where(low_half, xp, 0.0), jnp.where(low_half, 0.0, xp)],
                                      axis=0).astype(BF16)
                y_diag.append(_mm(lhs, rhs))
        decay = eacum_full[r]
        y = (jnp.concatenate(y_diag, axis=1) + jnp.concatenate(y_off, axis=1) * decay
             + dskip_ref[...] * xs[r])
        state = state * decay[q - 1:q, :] + jnp.concatenate(new_states, axis=1)
        y = y * zs_ref[r, :]
        o_ref[r, :] = _rms(y, ng_ref[...]).astype(o_ref.dtype)
    state_ref[...] = state


def _ssd(xbc, zs, dt, a_log, d_skip, norm_g):
    bsz, seq, xbc_w = xbc.shape
    q = SSM_CHUNK
    rows = SSD_CHUNKS_PER_STEP * q
    d_inner = SSM_HEADS * SSM_HEAD_DIM
    head_of_col = jnp.arange(d_inner) // SSM_HEAD_DIM
    expand = (jnp.arange(LANES)[:, None] == head_of_col[None, :]).astype(BF16)
    tri = (jnp.arange(q)[:, None] >= jnp.arange(q)[None, :]).astype(BF16)
    return pl.pallas_call(
        _ssd_kernel,
        out_shape=jax.ShapeDtypeStruct((bsz, seq, d_inner), BF16),
        grid=(bsz, seq // rows),
        in_specs=[pl.BlockSpec((None, rows, xbc_w), lambda bi, i: (bi, i, 0)),
                  pl.BlockSpec((None, rows, d_inner), lambda bi, i: (bi, i, 0)),
                  pl.BlockSpec((None, rows, LANES), lambda bi, i: (bi, i, 0)),
                  _resident((1, LANES)), _resident((1, d_inner)), _resident((1, d_inner)),
                  _resident((LANES, d_inner)), _resident((q, q))],
        out_specs=pl.BlockSpec((None, rows, d_inner), lambda bi, i: (bi, i, 0)),
        scratch_shapes=[pltpu.VMEM((SSM_STATE, d_inner), F32)],
        compiler_params=_params("parallel", "arbitrary"),
        name="ssd",
    )(xbc, zs, dt, jnp.pad(a_log, (0, LANES - SSM_HEADS)).reshape(1, LANES),
      jnp.repeat(d_skip, SSM_HEAD_DIM).reshape(1, d_inner), norm_g.reshape(1, d_inner), expand, tri)


def _proj_res_kernel(h_ref, xa_ref, xb_ref, w_ref, g_ref, o_ref):
    ka = xa_ref.shape[1]
    y = _mm(xa_ref[...], w_ref[:ka, :]) + _mm(xb_ref[...], w_ref[ka:, :])
    o_ref[...] = h_ref[...] + _rms(y, g_ref[...])


def _proj_res(h, xa, xb, w, which, g, tm=512):
    t, d = h.shape
    ka, kb = xa.shape[1], xb.shape[1]
    assert w.shape[-2] == ka + kb
    row = lambda n: pl.BlockSpec((tm, n), lambda i: (i, 0))
    return pl.pallas_call(
        _proj_res_kernel,
        out_shape=jax.ShapeDtypeStruct((t, d), F32),
        grid=(t // tm,),
        in_specs=[row(d), row(ka), row(kb), _stacked(w.shape, which), _resident((1, d))],
        out_specs=row(d),
        compiler_params=_params("parallel"),
        name="proj_res",
    )(h, xa, xb, w, g.reshape(1, d))


MAX_DILATION = 16
SPAN = ATTN_BLOCK * MAX_DILATION
ATTN_WIDTH = ATTN_HEADS * ATTN_HEAD_DIM


def _attend(jobs, bias_ref, scores_ref):
    dh = ATTN_HEAD_DIM
    nq = ATTN_BLOCK
    npairs = ATTN_HEADS // 2
    low_head = lax.broadcasted_iota(jnp.int32, (nq, LANES), 1) < dh
    for j, (q, k, _, _) in enumerate(jobs):
        for pair in range(npairs):
            cols = slice(pair * LANES, (pair + 1) * LANES)
            qp = q[:, cols]
            q2 = jnp.concatenate([jnp.where(low_head, qp, 0), jnp.where(low_head, 0, qp)], axis=0)
            scores_ref[j * npairs + pair] = lax.dot_general(k[:, cols], q2, (((1,), (1,)), ((), ())),
                                                            preferred_element_type=F32)
    results = []
    for j, (_, _, v, table) in enumerate(jobs):
        outs, lses = [], []
        for pair in range(npairs):
            cols = slice(pair * LANES, (pair + 1) * LANES)
            st = scores_ref[j * npairs + pair] + bias_ref[table, pair]
            m = jnp.max(st, axis=0, keepdims=True)
            p = jnp.exp(st - m)
            l = jnp.sum(p, axis=0, keepdims=True)
            ot = lax.dot_general(v[:, cols], p.astype(BF16), (((0,), (0,)), ((), ())), preferred_element_type=F32)
            ot = ot / l
            lse = m + jnp.log(l)
            for hh in range(2):
                outs.append(ot[hh * dh:(hh + 1) * dh, hh * nq:(hh + 1) * nq])
                lses.append(lse[:, hh * nq:(hh + 1) * nq])
        lse_t = jnp.concatenate(lses + [jnp.zeros((LANES - ATTN_HEADS, nq), F32)], axis=0)
        results.append((jnp.concatenate(outs, axis=0).T, lse_t.T))
    return results


MAX_BLOCKS_PER_STEP = 16


def _reset_carry(kprev_ref, vprev_ref):
    @pl.when(pl.program_id(2) == 0)
    def _():
        kprev_ref[...] = jnp.zeros(kprev_ref.shape, kprev_ref.dtype)
        vprev_ref[...] = jnp.zeros(vprev_ref.shape, vprev_ref.dtype)


def _attn_pairs_kernel(q_ref, kprev_ref, k_ref, vprev_ref, v_ref, bias_ref, o_ref, lse_ref, scores_ref):
    table = jnp.minimum(pl.program_id(2), 1)
    scale = ATTN_HEAD_DIM ** -0.5
    blocks = [slice(b * ATTN_BLOCK, (b + 1) * ATTN_BLOCK) for b in range(q_ref.shape[0] // ATTN_BLOCK)]
    jobs = [(q_ref[rows, :] * scale,
             jnp.concatenate([kprev_ref[rows, :], k_ref[rows, :]], axis=0),
             jnp.concatenate([vprev_ref[rows, :], v_ref[rows, :]], axis=0), table) for rows in blocks]
    for rows, (o, lse) in zip(blocks, _attend(jobs, bias_ref, scores_ref)):
        o_ref[rows, :] = o.astype(o_ref.dtype)
        lse_ref[rows, :] = lse


def _attn_chain_kernel(q_ref, k_ref, v_ref, bias_ref, o_ref, lse_ref, scores_ref, kprev_ref, vprev_ref, *, runs,
                       run_rows):
    _reset_carry(kprev_ref, vprev_ref)
    spans = q_ref.shape[0]
    part = ATTN_BLOCK // runs
    per_span = run_rows // part
    nb = spans * per_span
    scale = ATTN_HEAD_DIM ** -0.5

    def parts(ref):
        x = ref[...].astype(F32)
        return [x[s, :, p * part:(p + 1) * part, :].reshape(ATTN_BLOCK, x.shape[-1])
                for s in range(spans) for p in range(per_span)]

    k32, v32 = parts(k_ref), parts(v_ref)
    qs = [x.astype(BF16) for x in parts(q_ref)]
    ks = [x.astype(BF16) for x in k32]
    vs = [x.astype(BF16) for x in v32]
    kprev = [kprev_ref[...].astype(BF16)] + ks[:-1]
    vprev = [vprev_ref[...].astype(BF16)] + vs[:-1]
    first_table = jnp.minimum(pl.program_id(2), 1)
    jobs = [(qs[p] * scale, jnp.concatenate([kprev[p], ks[p]], axis=0),
             jnp.concatenate([vprev[p], vs[p]], axis=0), first_table if p == 0 else 1)
            for p in range(nb)]
    results = _attend(jobs, bias_ref, scores_ref)

    def interleave(xs):
        tiles = [x.reshape(runs, part, x.shape[-1]) for x in xs]
        return jnp.stack([jnp.concatenate(tiles[s * per_span:(s + 1) * per_span], axis=1)
                          for s in range(spans)], axis=0)

    o_ref[...] = interleave([o for o, _ in results]).astype(o_ref.dtype)
    lse_ref[...] = interleave([lse for _, lse in results])
    kprev_ref[...] = k32[-1]
    vprev_ref[...] = v32[-1]


def _attn_bias(window, dilation, offsets):
    steps = window // dilation
    slopes = jnp.exp2(-8.0 * jnp.arange(1, ATTN_HEADS + 1, dtype=F32) / ATTN_HEADS)
    kj = jnp.concatenate([offsets - ATTN_BLOCK, offsets])[:, None]
    qi = offsets[None, :]
    rel = qi - kj
    valid = (rel >= 0) & (rel <= steps)
    bias = -slopes[:, None, None] * (rel * dilation).astype(F32)
    later = jnp.where(valid[None], bias, MASK_VALUE)
    first = jnp.where((valid & (kj >= 0))[None], bias, MASK_VALUE)
    table = jnp.stack([first, later], axis=0)
    table = table.reshape(2, ATTN_HEADS // 2, 2, 2 * ATTN_BLOCK, ATTN_BLOCK)
    return table.transpose(0, 1, 3, 2, 4).reshape(2, ATTN_HEADS // 2, 2 * ATTN_BLOCK, 2 * ATTN_BLOCK)


def _attn_group(qkv, g, bsz, seq):
    window, dilation = DILATED_PATTERNS[g]
    ngroups = len(DILATED_PATTERNS)
    assert window // dilation <= ATTN_BLOCK and seq % SPAN == 0 and MAX_DILATION % dilation == 0
    t = bsz * seq
    nspan = seq // SPAN
    ncols = qkv.shape[1]
    col = lambda which: which * ngroups + g
    bias_spec = pl.BlockSpec((2, ATTN_HEADS // 2, 2 * ATTN_BLOCK, 2 * ATTN_BLOCK), lambda *_: (0, 0, 0, 0),
                             pipeline_mode=pl.Buffered(1))
    u = jnp.arange(ATTN_BLOCK)
    scores = lambda nb: pltpu.VMEM((nb * ATTN_HEADS // 2, 2 * ATTN_BLOCK, 2 * ATTN_BLOCK), F32)

    if dilation == MAX_DILATION:
        nb = MAX_BLOCKS_PER_STEP
        scores_scratch = scores(nb)
        groups = MAX_DILATION // nb
        rows = nb * ATTN_BLOCK
        cur = lambda which: pl.BlockSpec((rows, ATTN_WIDTH),
                                         lambda b, r, j: ((b * nspan + j) * groups + r, col(which)))
        prev = lambda which: pl.BlockSpec(
            (rows, ATTN_WIDTH), lambda b, r, j: ((b * nspan + jnp.maximum(j - 1, 0)) * groups + r, col(which)))
        out = lambda w: pl.BlockSpec((rows, w), lambda b, r, j: ((b * nspan + j) * groups + r, 0))
        o, lse = pl.pallas_call(
            _attn_pairs_kernel,
            out_shape=[jax.ShapeDtypeStruct((t, ATTN_WIDTH), BF16), jax.ShapeDtypeStruct((t, LANES), F32)],
            grid=(bsz, groups, nspan),
            in_specs=[cur(0), prev(1), cur(1), prev(2), cur(2), bias_spec],
            out_specs=[out(ATTN_WIDTH), out(LANES)],
            scratch_shapes=[scores_scratch],
            compiler_params=_params("parallel", "parallel", "arbitrary"),
            name=f"attn_d{dilation}",
        )(qkv, qkv, qkv, qkv, qkv, _attn_bias(window, dilation, u))
        return o, lse

    runs = MAX_DILATION // dilation
    per_span = min(MAX_BLOCKS_PER_STEP, runs)
    spans = min(nspan, MAX_BLOCKS_PER_STEP // per_span) if per_span == runs else 1
    assert nspan % spans == 0
    part = ATTN_BLOCK // runs
    run_rows = per_span * part
    steps_per_span = ATTN_BLOCK // run_rows
    shape = (bsz, nspan, runs, dilation, steps_per_span, run_rows)
    blk = lambda w: (None, spans, runs, None, None, run_rows, w)
    at = lambda b, r, s, c: (b, s // steps_per_span, 0, r, s % steps_per_span, 0, c)
    cur = lambda which: pl.BlockSpec(blk(ATTN_WIDTH), lambda b, r, s: at(b, r, s, col(which)))
    out = lambda w: pl.BlockSpec(blk(w), lambda b, r, s: at(b, r, s, 0))
    view = qkv.reshape(shape + (ncols,))
    offsets = runs * (u % part) + u // part
    carry = pltpu.VMEM((ATTN_BLOCK, ATTN_WIDTH), F32)
    o, lse = pl.pallas_call(
        functools.partial(_attn_chain_kernel, runs=runs, run_rows=run_rows),
        out_shape=[jax.ShapeDtypeStruct(shape + (ATTN_WIDTH,), BF16), jax.ShapeDtypeStruct(shape + (LANES,), F32)],
        grid=(bsz, dilation, nspan // spans * steps_per_span),
        in_specs=[cur(0), cur(1), cur(2), bias_spec],
        out_specs=[out(ATTN_WIDTH), out(LANES)],
        scratch_shapes=[scores(spans * per_span), carry, carry],
        compiler_params=_params("parallel", "parallel", "arbitrary"),
        name=f"attn_d{dilation}",
    )(view, view, view, _attn_bias(window, dilation, offsets))
    return o.reshape(t, ATTN_WIDTH), lse.reshape(t, LANES)


def _merge_proj_kernel(h_ref, o0_ref, o1_ref, o2_ref, l0_ref, l1_ref, l2_ref, expand_ref, w_ref, g_ref, out_ref):
    l0, l1, l2 = l0_ref[...], l1_ref[...], l2_ref[...]
    m = jnp.maximum(jnp.maximum(l0, l1), l2)
    e0, e1, e2 = jnp.exp(l0 - m), jnp.exp(l1 - m), jnp.exp(l2 - m)
    inv = 1.0 / (e0 + e1 + e2)
    merged = None
    for e, o_ref in ((e0, o0_ref), (e1, o1_ref), (e2, o2_ref)):
        wgt = e * inv
        w_hi = wgt.astype(BF16)
        w_lo = (wgt - w_hi.astype(F32)).astype(BF16)
        term = (_mm(w_hi, expand_ref[...]) + _mm(w_lo, expand_ref[...])) * o_ref[...]
        merged = term if merged is None else merged + term
    y = _rms(_mm(merged.astype(BF16), w_ref[...]), g_ref[...])
    for r in range(out_ref.shape[1]):
        out_ref[:, r, :] = y[r * ATTN_BLOCK:(r + 1) * ATTN_BLOCK, :]
    out_ref[...] = out_ref[...] + h_ref[...]


def _merge_proj(h, outs, lses, w, which, g):
    t, d = h.shape
    k = w.shape[-2]
    rows = RESIDUES_PER_STEP * ATTN_BLOCK
    per_span = MAX_DILATION // RESIDUES_PER_STEP
    row = lambda n: pl.BlockSpec((rows, n), lambda i: (i, 0))
    tokens = pl.BlockSpec((None, ATTN_BLOCK, RESIDUES_PER_STEP, d), lambda i: (i // per_span, 0, i % per_span, 0))
    head_of_col = jnp.arange(k) // ATTN_HEAD_DIM
    expand = (jnp.arange(LANES)[:, None] == head_of_col[None, :]).astype(BF16)
    view = (t // SPAN, ATTN_BLOCK, MAX_DILATION, d)
    return pl.pallas_call(
        _merge_proj_kernel,
        out_shape=jax.ShapeDtypeStruct(view, F32),
        grid=(t // rows,),
        in_specs=[tokens] + [row(k)] * 3 + [row(LANES)] * 3
        + [_resident((LANES, k)), _stacked(w.shape, which), _resident((1, d))],
        out_specs=tokens,
        compiler_params=_params("parallel"),
        name="merge_proj",
    )(h.reshape(view), *outs, *lses, expand, w, g.reshape(1, d)).reshape(t, d)


def _conv_ssd_mixer(h, bsz, seq, g_pre, g_post, w_in, j, conv_dw_w, conv_dw_b, conv_ln_g, conv_ln_b,
                    ssm_conv_w, ssm_conv_b, ssm_dt_bias, ssm_a_log, ssm_d, ssm_norm_g, w_out):
    conv_ch = conv_dw_w.shape[1]
    d_inner = SSM_HEADS * SSM_HEAD_DIM
    xbc_w = ssm_conv_w.shape[1]
    xbc, glu, zs, dt = _hyb_in(h, g_pre, w_in, (j,), conv_ch, d_inner, ssm_conv_w, ssm_conv_b, ssm_dt_bias, seq)
    ya = _convmod(glu.reshape(bsz, seq, conv_ch), conv_dw_w, conv_dw_b, conv_ln_g, conv_ln_b)
    yb = _ssd(xbc.reshape(bsz, seq, xbc_w), zs.reshape(bsz, seq, d_inner), dt.reshape(bsz, seq, LANES),
              ssm_a_log, ssm_d, ssm_norm_g)
    t = bsz * seq
    return _proj_res(h, ya.reshape(t, conv_ch), yb.reshape(t, d_inner), w_out, (j,), g_post)


def _attention_mixer(h, bsz, seq, g_pre, g_post, w_qkv, w_o, j):
    qkv = _qkv_proj(h, g_pre, w_qkv, (j,))
    outs, lses = [], []
    for g in range(len(DILATED_PATTERNS)):
        o, lse = _attn_group(qkv, g, bsz, seq)
        outs.append(o)
        lses.append(lse)
    return _merge_proj(h, outs, lses, w_o, (j,), g_post)


def kernel(x, norm_g, ffn_w1, ffn_w2, hyb_w_in, conv_dw_w, conv_dw_b, conv_ln_g, conv_ln_b, ssm_conv_w, ssm_conv_b, ssm_dt_bias, ssm_a_log, ssm_d, ssm_norm_g, hyb_w_out, attn_w_qkv, attn_w_o):
    bsz, seq, d = x.shape
    h = x.reshape(bsz * seq, d)
    w1, w2 = ffn_w1.astype(BF16), ffn_w2.astype(BF16)
    w_in = jnp.pad(hyb_w_in, ((0, 0), (0, 0), (0, LANES - SSM_HEADS))).astype(BF16)
    w_out, w_qkv, w_o = hyb_w_out.astype(BF16), attn_w_qkv.astype(BF16), attn_w_o.astype(BF16)
    for i in range(norm_g.shape[0]):
        g = norm_g[i]
        j = i // 2
        h = _ffn(h, g[0], g[1], w1, w2, (i, 0))
        if i % 2 == 0:
            h = _conv_ssd_mixer(h, bsz, seq, g[2], g[3], w_in, j, conv_dw_w[j], conv_dw_b[j],
                                conv_ln_g[j], conv_ln_b[j], ssm_conv_w[j], ssm_conv_b[j], ssm_dt_bias[j],
                                ssm_a_log[j], ssm_d[j], ssm_norm_g[j], w_out)
        else:
            h = _attention_mixer(h, bsz, seq, g[2], g[3], w_qkv, w_o, j)
        h = _ffn(h, g[4], g[5], w1, w2, (i, 1))
    return h.reshape(bsz, seq, d)
```
